```python
import jax, jax.numpy as jnp
from jax import lax
import numpy as np

D_MODEL = 2048
BATCH = 1
SEQ = 16384
DEPTH = 2

N_A_LAYERS = DEPTH // 2
N_B_LAYERS = DEPTH - N_A_LAYERS
HEAD_DIM = 128
MIX_WIDTH = D_MODEL
MEM_TOKENS = 256
MEM_HEADS = 4
MEM_WIDTH = MEM_HEADS * HEAD_DIM
LRU_WIDTH = MIX_WIDTH - MEM_WIDTH
LRU_BLOCKS = LRU_WIDTH // HEAD_DIM
LRU_BLOCK = LRU_WIDTH // LRU_BLOCKS
CONV_WIDTH = 4
LRU_C = 8.0
SB_HEADS = (MIX_WIDTH - MEM_WIDTH) // HEAD_DIM
SB_WIDTH = SB_HEADS * HEAD_DIM
Q_BLOCK = 128
N_EXPERTS = 32
TOP_K = 4
D_FF = D_MODEL
SWIGLU_LIMIT = 7.0
SWIGLU_ALPHA = 1.702
EXPERT_CHUNK = 512
DN_ALPHA = (2 * DEPTH) ** 0.25
DN_BETA = (8 * DEPTH) ** -0.25
LN_EPS = 1e-5

kernel_name = "hybrid_rglru_stickbreak_moe_deepnorm"

F32 = jnp.float32


def layer_norm(h, g, b):
    hf = h.astype(F32)
    mu = jnp.mean(hf, axis=-1, keepdims=True)
    var = jnp.mean(jnp.square(hf - mu), axis=-1, keepdims=True)
    y = (hf - mu) * lax.rsqrt(var + LN_EPS) * g.astype(F32) + b.astype(F32)
    return y.astype(h.dtype)


def memory_attention(q, mem_kv):
    B, S, _ = q.shape
    M = mem_kv.shape[1]
    qh = q.reshape(B, S, MEM_HEADS, HEAD_DIM)
    k = mem_kv[..., :MEM_WIDTH].reshape(B, M, MEM_HEADS, HEAD_DIM)
    v = mem_kv[..., MEM_WIDTH:].reshape(B, M, MEM_HEADS, HEAD_DIM)
    s = jnp.einsum('bshd,bmhd->bhsm', qh.astype(F32), k.astype(F32)) * (HEAD_DIM ** -0.5)
    p = jax.nn.softmax(s, axis=-1)
    o = jnp.einsum('bhsm,bmhd->bshd', p, v.astype(F32))
    return o.reshape(B, S, MEM_WIDTH).astype(q.dtype)


def causal_depthwise_conv(u, w, bias):
    S = u.shape[1]
    up = jnp.pad(u, ((0, 0), (CONV_WIDTH - 1, 0), (0, 0)))
    out = bias
    for tap in range(CONV_WIDTH):
        out = out + up[:, tap:tap + S] * w[tap]
    return out


def _linear_recurrence_combine(left, right):
    a_l, b_l = left
    a_r, b_r = right
    return a_l * a_r, a_r * b_l + b_r


def rg_lru(xc, rg_w, rg_b, ig_w, ig_b, lam):
    B, S, W = xc.shape
    xf = xc.astype(F32)
    xb = xf.reshape(B, S, LRU_BLOCKS, LRU_BLOCK)
    r = jax.nn.sigmoid(jnp.einsum('bsnc,ncd->bsnd', xb, rg_w.astype(F32)).reshape(B, S, W) + rg_b.astype(F32))
    i = jax.nn.sigmoid(jnp.einsum('bsnc,ncd->bsnd', xb, ig_w.astype(F32)).reshape(B, S, W) + ig_b.astype(F32))
    log_a = -LRU_C * r * jax.nn.softplus(-lam.astype(F32))
    a = jnp.exp(log_a)
    b = jnp.sqrt(-jnp.expm1(2.0 * log_a)) * (i * xf)
    _, h = lax.associative_scan(_linear_recurrence_combine, (a, b), axis=1)
    return h.astype(xc.dtype)


def rglru_layer_mixer(h, mem_kv, w_in, conv_w, conv_b, rg_w, rg_b, ig_w, ig_b, lam, w_out):
    proj = h @ w_in
    gate_branch = proj[..., :LRU_WIDTH]
    lru_in = proj[..., LRU_WIDTH:2 * LRU_WIDTH]
    q_mem = proj[..., 2 * LRU_WIDTH:]
    xc = causal_depthwise_conv(lru_in, conv_w, conv_b)
    lru_out = rg_lru(xc, rg_w, rg_b, ig_w, ig_b, lam) * jax.nn.gelu(gate_branch)
    mem_out = memory_attention(q_mem, mem_kv)
    return jnp.concatenate([lru_out, mem_out], axis=-1) @ w_out


def stick_breaking_attention(q, k, v):
    B, S, H, d = q.shape
    n_blk = S // Q_BLOCK
    qb = q.astype(F32).reshape(B, n_blk, Q_BLOCK, H, d).transpose(1, 0, 3, 2, 4)
    kf = k.astype(F32).transpose(0, 2, 1, 3)
    vf = v.astype(F32).transpose(0, 2, 1, 3)
    kpos = jnp.arange(S)
    scale = d ** -0.5

    def one_block(args):
        qi, q0 = args
        z = jnp.einsum('bhqd,bhkd->bhqk', qi, kf) * scale
        qpos = q0 + jnp.arange(Q_BLOCK)
        causal = kpos[None, :] < qpos[:, None]
        log_keep = jnp.where(causal, jax.nn.log_sigmoid(-z), 0.0)
        after = lax.cumsum(log_keep, axis=3, reverse=True) - log_keep
        wts = jnp.where(causal, jnp.exp(jax.nn.log_sigmoid(z) + after), 0.0)
        return jnp.einsum('bhqk,bhkd->bqhd', wts, vf)

    out = lax.map(one_block, (qb, jnp.arange(n_blk) * Q_BLOCK))
    return out.transpose(1, 0, 2, 3, 4).reshape(B, S, H * d).astype(q.dtype)


def stickbreak_layer_mixer(h, kv_shared, mem_kv, w_q, w_out):
    B, S, _ = h.shape
    proj = h @ w_q
    q_sb = proj[..., :SB_WIDTH].reshape(B, S, SB_HEADS, HEAD_DIM)
    q_mem = proj[..., SB_WIDTH:]
    k = kv_shared[..., :SB_WIDTH].reshape(B, S, SB_HEADS, HEAD_DIM)
    v = kv_shared[..., SB_WIDTH:].reshape(B, S, SB_HEADS, HEAD_DIM)
    sb_out = stick_breaking_attention(q_sb, k, v)
    mem_out = memory_attention(q_mem, mem_kv)
    return jnp.concatenate([sb_out, mem_out], axis=-1) @ w_out


def moe_ffn(h, router_w, router_b, w_gate_up, b_gate_up, w_down, b_down):
    B, S, D = h.shape
    tok = h.reshape(-1, D)
    T = tok.shape[0]
    TK = T * TOP_K
    logits = (tok @ router_w).astype(F32) + router_b.astype(F32)
    top_v, top_e = lax.top_k(logits, TOP_K)
    gates = jax.nn.softmax(top_v, axis=-1)
    flat_e = top_e.reshape(-1).astype(jnp.int32)
    flat_tok = (jnp.arange(TK, dtype=jnp.int32) // TOP_K)
    flat_g = gates.reshape(-1)
    order = jnp.argsort(flat_e)
    s_e, s_tok, s_g = flat_e[order], flat_tok[order], flat_g[order]
    counts = jnp.bincount(flat_e, length=N_EXPERTS).astype(jnp.int32)
    padded = (counts + EXPERT_CHUNK - 1) // EXPERT_CHUNK * EXPERT_CHUNK
    group_start = jnp.cumsum(counts) - counts
    padded_end = jnp.cumsum(padded)
    padded_start = padded_end - padded
    dest = padded_start[s_e] + jnp.arange(TK, dtype=jnp.int32) - group_start[s_e]
    n_chunks = -(-TK // EXPERT_CHUNK) + N_EXPERTS
    buf_tok = jnp.zeros((n_chunks * EXPERT_CHUNK,), jnp.int32).at[dest].set(s_tok)
    buf_g = jnp.zeros((n_chunks * EXPERT_CHUNK,), F32).at[dest].set(s_g)
    chunk_e = jnp.minimum(
        jnp.searchsorted(padded_end, jnp.arange(n_chunks, dtype=jnp.int32) * EXPERT_CHUNK, side='right'),
        N_EXPERTS - 1)

    def run_chunk(args):
        idx, g, e = args
        xe = tok[idx]
        gu = (xe @ w_gate_up[e] + b_gate_up[e]).astype(F32)
        gate = jnp.minimum(gu[..., 0::2], SWIGLU_LIMIT)
        up = jnp.clip(gu[..., 1::2], -SWIGLU_LIMIT, SWIGLU_LIMIT)
        act = (up + 1.0) * gate * jax.nn.sigmoid(SWIGLU_ALPHA * gate)
        y = (act.astype(xe.dtype) @ w_down[e] + b_down[e]).astype(F32)
        return y * g[:, None]

    rows = lax.map(run_chunk, (buf_tok.reshape(n_chunks, EXPERT_CHUNK),
                               buf_g.reshape(n_chunks, EXPERT_CHUNK), chunk_e))
    out = jnp.zeros((T, D), F32).at[buf_tok].add(rows.reshape(-1, D))
    return out.reshape(B, S, D).astype(h.dtype)


def setup_inputs(seed: int = 0) -> dict:
    key = jax.random.key(seed)
    ks = jax.random.split(key, 32)
    D = D_MODEL

    def nrm(k, shape, scale):
        return jax.random.normal(k, shape, F32) * scale

    a_u = jax.random.uniform(ks[10], (N_A_LAYERS, LRU_WIDTH), F32, 0.9, 0.999)
    a_base = a_u ** (1.0 / LRU_C)
    return {
        "x": nrm(ks[0], (BATCH, SEQ, D), 1.0),
        "mem": nrm(ks[1], (BATCH, MEM_TOKENS, D), 1.0),
        "a_w_in": nrm(ks[2], (N_A_LAYERS, D, 2 * LRU_WIDTH + MEM_WIDTH), D ** -0.5),
        "a_conv_w": nrm(ks[3], (N_A_LAYERS, CONV_WIDTH, LRU_WIDTH), CONV_WIDTH ** -0.5),
        "a_conv_b": nrm(ks[4], (N_A_LAYERS, LRU_WIDTH), 0.01),
        "a_rg_w": nrm(ks[5], (N_A_LAYERS, LRU_BLOCKS, LRU_BLOCK, LRU_BLOCK), LRU_BLOCK ** -0.5),
        "a_rg_b": nrm(ks[6], (N_A_LAYERS, LRU_WIDTH), 0.01),
        "a_ig_w": nrm(ks[7], (N_A_LAYERS, LRU_BLOCKS, LRU_BLOCK, LRU_BLOCK), LRU_BLOCK ** -0.5),
        "a_ig_b": nrm(ks[8], (N_A_LAYERS, LRU_WIDTH), 0.01),
        "a_lambda": jnp.log(a_base) - jnp.log1p(-a_base),
        "a_w_out": nrm(ks[9], (N_A_LAYERS, LRU_WIDTH + MEM_WIDTH, D), (LRU_WIDTH + MEM_WIDTH) ** -0.5 * DN_BETA),
        "b_w_q": nrm(ks[11], (N_B_LAYERS, D, SB_WIDTH + MEM_WIDTH), D ** -0.5),
        "b_w_out": nrm(ks[12], (N_B_LAYERS, SB_WIDTH + MEM_WIDTH, D), (SB_WIDTH + MEM_WIDTH) ** -0.5 * DN_BETA),
        "w_kv_shared": nrm(ks[13], (D, 2 * SB_WIDTH), D ** -0.5),
        "mem_w_kv": nrm(ks[14], (DEPTH, D, 2 * MEM_WIDTH), D ** -0.5),
        "ln1_g": 1.0 + nrm(ks[15], (DEPTH, D), 0.02),
        "ln1_b": nrm(ks[16], (DEPTH, D), 0.02),
        "ln2_g": 1.0 + nrm(ks[17], (DEPTH, D), 0.02),
        "ln2_b": nrm(ks[18], (DEPTH, D), 0.02),
        "router_w": nrm(ks[19], (DEPTH, D, N_EXPERTS), D ** -0.5),
        "router_b": nrm(ks[20], (DEPTH, N_EXPERTS), 0.01),
        "w_gate_up": nrm(ks[21], (DEPTH, N_EXPERTS, D, 2 * D_FF), D ** -0.5),
        "b_gate_up": nrm(ks[22], (DEPTH, N_EXPERTS, 2 * D_FF), 0.01),
        "w_down": nrm(ks[23], (DEPTH, N_EXPERTS, D_FF, D), D_FF ** -0.5 * DN_BETA),
        "b_down": nrm(ks[24], (DEPTH, N_EXPERTS, D), 0.01),
    }


def reference(x, mem, a_w_in, a_conv_w, a_conv_b, a_rg_w, a_rg_b, a_ig_w, a_ig_b, a_lambda, a_w_out,
              b_w_q, b_w_out, w_kv_shared, mem_w_kv, ln1_g, ln1_b, ln2_g, ln2_b,
              router_w, router_b, w_gate_up, b_gate_up, w_down, b_down):
    h = x
    kv_shared = None
    for layer in range(DEPTH):
        mem_kv = mem @ mem_w_kv[layer]
        if layer < N_A_LAYERS:
            i = layer
            mix = rglru_layer_mixer(h, mem_kv, a_w_in[i], a_conv_w[i], a_conv_b[i], a_rg_w[i], a_rg_b[i],
                                    a_ig_w[i], a_ig_b[i], a_lambda[i], a_w_out[i])
        else:
            if layer == N_A_LAYERS:
                kv_shared = h @ w_kv_shared
            j = layer - N_A_LAYERS
            mix = stickbreak_layer_mixer(h, kv_shared, mem_kv, b_w_q[j], b_w_out[j])
        h = layer_norm(DN_ALPHA * h + mix, ln1_g[layer], ln1_b[layer])
        ffn = moe_ffn(h, router_w[layer], router_b[layer], w_gate_up[layer], b_gate_up[layer],
                      w_down[layer], b_down[layer])
        h = layer_norm(DN_ALPHA * h + ffn, ln2_g[layer], ln2_b[layer])
    return h
```

```python
import functools

import jax
import jax.numpy as jnp
from jax import lax
from jax.experimental import pallas as pl
from jax.experimental.pallas import tpu as pltpu

F32 = jnp.float32
BF16 = jnp.bfloat16
I32 = jnp.int32

HEAD_DIM = 128
MEM_HEADS = 4
MEM_WIDTH = MEM_HEADS * HEAD_DIM
CONV_WIDTH = 4
LRU_C = 8.0
N_EXPERTS = 32
TOP_K = 4
SWIGLU_LIMIT = 7.0
SWIGLU_ALPHA = 1.702
LN_EPS = 1e-5
LANES = 128
SUBLANES = 8
VMEM_LIMIT = 56 * 1024 * 1024
SB_SKIP = 100.0


def _params(n_axes, vmem=VMEM_LIMIT):
    return pltpu.CompilerParams(dimension_semantics=("arbitrary",) * n_axes, vmem_limit_bytes=vmem)


def _dense_kernel(x_ref, w_ref, o_ref, xb_ref, *, tn):
    @pl.when(pl.program_id(1) == 0)
    def _():
        xb_ref[...] = x_ref[...].astype(BF16)

    xb = xb_ref[...]
    for n in range(0, w_ref.shape[1], tn):
        o_ref[:, n:n + tn] = jnp.dot(xb, w_ref[:, n:n + tn], preferred_element_type=F32).astype(o_ref.dtype)


def dense(x, w, out_dtype, *, tm, tnb, tn=256):
    m, k = x.shape
    n = w.shape[1]
    return pl.pallas_call(
        functools.partial(_dense_kernel, tn=tn),
        out_shape=jax.ShapeDtypeStruct((m, n), out_dtype),
        grid=(m // tm, n // tnb),
        in_specs=[pl.BlockSpec((tm, k), lambda i, j: (i, 0)),
                  pl.BlockSpec((k, tnb), lambda i, j: (0, j))],
        out_specs=pl.BlockSpec((tm, tnb), lambda i, j: (i, j)),
        scratch_shapes=[pltpu.VMEM((tm, k), BF16)],
        compiler_params=_params(2),
        name="dense",
    )(x, w)


def _softplus(x):
    return jnp.maximum(x, 0.0) + jnp.log1p(jnp.exp(-jnp.abs(x)))


def _gelu_tanh(x):
    return 0.5 * x * (1.0 + jnp.tanh(0.7978845608028654 * (x + 0.044715 * (x * x * x))))


def _rglru_kernel(gate_ref, u_ref, cw_ref, cb_ref, rgw_ref, rgb_ref, igw_ref, igb_ref, lam_ref,
                  o_ref, hist_ref, carry_ref, *, tt):
    t = pl.program_id(1)

    @pl.when(t == 0)
    def _():
        hist_ref[...] = jnp.zeros_like(hist_ref)
        carry_ref[...] = jnp.zeros_like(carry_ref)

    u = u_ref[...]
    ext = jnp.concatenate([hist_ref[...], u], axis=0)
    cw = cw_ref[...]
    xc = cb_ref[...] + u * cw[CONV_WIDTH - 1:CONV_WIDTH, :]
    for d in range(1, CONV_WIDTH):
        shifted = pltpu.roll(ext, d, axis=0)[SUBLANES:, :]
        xc = xc + shifted * cw[CONV_WIDTH - 1 - d:CONV_WIDTH - d, :]
    hist_ref[...] = u[tt - SUBLANES:, :]

    xcb = xc.astype(BF16)
    r = jax.nn.sigmoid(jnp.dot(xcb, rgw_ref[...], preferred_element_type=F32) + rgb_ref[...])
    ig = jax.nn.sigmoid(jnp.dot(xcb, igw_ref[...], preferred_element_type=F32) + igb_ref[...])
    log_a = (-LRU_C) * r * _softplus(-lam_ref[...])
    a = jnp.exp(log_a)
    b = jnp.sqrt(-jnp.tanh(log_a) * (a * a + 1.0)) * (ig * xc)

    row = lax.broadcasted_iota(I32, (tt, LANES), 0)
    d = 1
    while d < tt:
        keep = row >= d
        a_sh = jnp.where(keep, pltpu.roll(a, d, axis=0), 1.0)
        b_sh = jnp.where(keep, pltpu.roll(b, d, axis=0), 0.0)
        b = a * b_sh + b
        a = a * a_sh
        d *= 2
    h = a * carry_ref[...] + b
    carry_ref[...] = h[tt - 1:tt, :]
    o_ref[...] = (h * _gelu_tanh(gate_ref[...])).astype(o_ref.dtype)


def rglru(proj, conv_w, conv_b, rg_w, rg_b, ig_w, ig_b, lam, *, tt=512):
    t_len = proj.shape[0]
    nblk = rg_w.shape[0]
    width = nblk * LANES
    vec = lambda v: v.reshape(1, width)
    chan = lambda rows: pl.BlockSpec((rows, LANES), lambda n, t: (0, n))
    return pl.pallas_call(
        functools.partial(_rglru_kernel, tt=tt),
        out_shape=jax.ShapeDtypeStruct((t_len, width), BF16),
        grid=(nblk, t_len // tt),
        in_specs=[pl.BlockSpec((tt, LANES), lambda n, t: (t, n)),
                  pl.BlockSpec((tt, LANES), lambda n, t: (t, nblk + n)),
                  chan(CONV_WIDTH), chan(1),
                  pl.BlockSpec((None, LANES, LANES), lambda n, t: (n, 0, 0)), chan(1),
                  pl.BlockSpec((None, LANES, LANES), lambda n, t: (n, 0, 0)), chan(1),
                  chan(1)],
        out_specs=pl.BlockSpec((tt, LANES), lambda n, t: (t, n)),
        scratch_shapes=[pltpu.VMEM((SUBLANES, LANES), F32), pltpu.VMEM((1, LANES), F32)],
        compiler_params=_params(2),
        name="rglru",
    )(proj, proj, conv_w, vec(conv_b), rg_w.astype(BF16), vec(rg_b), ig_w.astype(BF16), vec(ig_b), vec(lam))


def _sb_kernel(q_ref, k_ref, v_ref, o_ref, acc_ref, run_ref, *, tq):
    i = pl.program_id(1)
    q = q_ref[...]
    scale = HEAD_DIM ** -0.5
    rows = lax.broadcasted_iota(I32, (tq, tq), 0)
    cols = lax.broadcasted_iota(I32, (tq, tq), 1)
    later = (rows > cols).astype(BF16)
    causal = cols < rows

    acc_ref[...] = jnp.zeros_like(acc_ref)
    run_ref[...] = jnp.zeros_like(run_ref)

    def block(j, masked):
        start = pl.multiple_of(j * tq, tq)
        kb = k_ref[pl.ds(start, tq), :]
        vb = v_ref[pl.ds(start, tq), :]
        z = lax.dot_general(q, kb, (((1,), (1,)), ((), ())), preferred_element_type=F32) * scale
        sp = _softplus(z)
        log_keep = -sp
        log_beta = z - sp
        if masked:
            log_keep = jnp.where(causal, log_keep, 0.0)
        hi = log_keep.astype(BF16)
        lo = (log_keep - hi.astype(F32)).astype(BF16)
        after = (jnp.dot(hi, later, preferred_element_type=F32)
                 + jnp.dot(lo, later, preferred_element_type=F32))
        run = run_ref[...]
        w = jnp.exp(log_beta + after + run)
        if masked:
            w = jnp.where(causal, w, 0.0)
        acc_ref[...] += jnp.dot(w.astype(BF16), vb, preferred_element_type=F32)
        run = run + after[:, 0:1] + log_keep[:, 0:1]
        run_ref[...] = run
        return jnp.max(run)

    first = block(i, True)

    def cond(state):
        j, top = state
        return jnp.logical_and(j >= 0, top > -SB_SKIP)

    def body(state):
        j, _ = state
        return j - 1, block(j, False)

    lax.while_loop(cond, body, (i - 1, first))
    o_ref[...] = acc_ref[...].astype(o_ref.dtype)


def stick_breaking(qkv, *, n_heads, q_col, k_col, v_col, tq=256):
    t_len = qkv.shape[0]
    return pl.pallas_call(
        functools.partial(_sb_kernel, tq=tq),
        out_shape=jax.ShapeDtypeStruct((t_len, n_heads * HEAD_DIM), BF16),
        grid=(n_heads, t_len // tq),
        in_specs=[pl.BlockSpec((tq, HEAD_DIM), lambda h, i: (i, q_col + h)),
                  pl.BlockSpec((t_len, HEAD_DIM), lambda h, i: (0, k_col + h)),
                  pl.BlockSpec((t_len, HEAD_DIM), lambda h, i: (0, v_col + h))],
        out_specs=pl.BlockSpec((tq, HEAD_DIM), lambda h, i: (i, h)),
        scratch_shapes=[pltpu.VMEM((tq, HEAD_DIM), F32), pltpu.VMEM((tq, 1), F32)],
        compiler_params=_params(2),
        name="stick_breaking",
    )(qkv, qkv, qkv)


def _layer_norm(y, g, b):
    mu = jnp.mean(y, axis=-1, keepdims=True)
    yc = y - mu
    var = jnp.mean(yc * yc, axis=-1, keepdims=True)
    return yc * lax.rsqrt(var + LN_EPS) * g + b


def _mixout_kernel(main_ref, qm_ref, mkv_ref, wout_ref, h_ref, g_ref, b_ref, rwh_ref, rwl_ref, rb_ref,
                   o_ref, logit_ref, *, alpha):
    main_w = main_ref.shape[1]
    mix = jnp.dot(main_ref[...], wout_ref[0:main_w, :], preferred_element_type=F32)
    q = qm_ref[...].astype(BF16)
    heads = []
    for hd in range(MEM_HEADS):
        lo, hi = hd * HEAD_DIM, (hd + 1) * HEAD_DIM
        kh = mkv_ref[:, lo:hi]
        vh = mkv_ref[:, MEM_WIDTH + lo:MEM_WIDTH + hi]
        s = lax.dot_general(q[:, lo:hi], kh, (((1,), (1,)), ((), ())),
                            preferred_element_type=F32) * (HEAD_DIM ** -0.5)
        e = jnp.exp(s - jnp.max(s, axis=-1, keepdims=True))
        p = e * (1.0 / jnp.sum(e, axis=-1, keepdims=True))
        heads.append(jnp.dot(p.astype(BF16), vh, preferred_element_type=F32).astype(BF16))
    mem_out = jnp.concatenate(heads, axis=-1)
    mix = mix + jnp.dot(mem_out, wout_ref[main_w:, :], preferred_element_type=F32)
    h1 = _layer_norm(alpha * h_ref[...] + mix, g_ref[...], b_ref[...])
    o_ref[...] = h1
    hh = h1.astype(BF16)
    hl = (h1 - hh.astype(F32)).astype(BF16)
    logit_ref[...] = (jnp.dot(hh, rwh_ref[...], preferred_element_type=F32)
                      + jnp.dot(hl, rwh_ref[...], preferred_element_type=F32)
                      + jnp.dot(hh, rwl_ref[...], preferred_element_type=F32) + rb_ref[...])


def mixout(main, qsrc, q_col, mem_kv, w_out, h, ln_g, ln_b, router_w, router_b, *, alpha, tm=256):
    t_len, d = h.shape
    main_w = main.shape[1]
    n_e = router_w.shape[1]
    rw = jnp.zeros((d, LANES), F32).at[:, :n_e].set(router_w)
    rwh = rw.astype(BF16)
    rwl = (rw - rwh.astype(F32)).astype(BF16)
    rb = jnp.full((1, LANES), -jnp.inf, F32).at[0, :n_e].set(router_b)
    full = lambda a: pl.BlockSpec(a.shape, lambda i: (0, 0))
    args = (main, qsrc, mem_kv, w_out.astype(BF16), h, ln_g.reshape(1, d), ln_b.reshape(1, d), rwh, rwl, rb)
    return pl.pallas_call(
        functools.partial(_mixout_kernel, alpha=alpha),
        out_shape=(jax.ShapeDtypeStruct((t_len, d), F32), jax.ShapeDtypeStruct((t_len, LANES), F32)),
        grid=(t_len // tm,),
        in_specs=[pl.BlockSpec((tm, main_w), lambda i: (i, 0)),
                  pl.BlockSpec((tm, MEM_WIDTH), lambda i: (i, q_col)),
                  full(mem_kv), full(args[3]),
                  pl.BlockSpec((tm, d), lambda i: (i, 0)),
                  full(args[5]), full(args[6]), full(rwh), full(rwl), full(rb)],
        out_specs=(pl.BlockSpec((tm, d), lambda i: (i, 0)), pl.BlockSpec((tm, LANES), lambda i: (i, 0))),
        compiler_params=_params(1),
        name="mixout",
    )(*args)


def _router_kernel(logit_ref, route_ref, cnt_ref, carry_ref, *, tb):
    @pl.when(pl.program_id(0) == 0)
    def _():
        carry_ref[...] = jnp.zeros_like(carry_ref)

    vals = logit_ref[...]
    lane = lax.broadcasted_iota(I32, (tb, LANES), 1).astype(F32)
    sels, tops, ids = [], [], []
    for _ in range(TOP_K):
        m = jnp.max(vals, axis=-1, keepdims=True)
        idx = jnp.min(jnp.where(vals == m, lane, float(LANES)), axis=-1, keepdims=True)
        sel = lane == idx
        vals = jnp.where(sel, -jnp.inf, vals)
        sels.append(sel)
        tops.append(m)
        ids.append(idx)
    exps = [jnp.exp(m - tops[0]) for m in tops]
    inv = 1.0 / (exps[0] + exps[1] + exps[2] + exps[3])
    onehot = jnp.zeros((tb, LANES), F32)
    for sel in sels:
        onehot = onehot + sel.astype(F32)
    rows = lax.broadcasted_iota(I32, (tb, tb), 0)
    cols = lax.broadcasted_iota(I32, (tb, tb), 1)
    before = (cols < rows).astype(BF16)
    cum = jnp.dot(before, onehot.astype(BF16), preferred_element_type=F32) + carry_ref[...]
    out = jnp.zeros((tb, LANES), F32)
    for k in range(TOP_K):
        pos = jnp.sum(jnp.where(sels[k], cum, 0.0), axis=-1, keepdims=True)
        out = jnp.where(lane == k, ids[k], out)
        out = jnp.where(lane == TOP_K + k, exps[k] * inv, out)
        out = jnp.where(lane == 2 * TOP_K + k, pos, out)
    route_ref[...] = out
    carry_ref[...] += jnp.sum(onehot, axis=0, keepdims=True)
    cnt_ref[...] = jnp.broadcast_to(carry_ref[...], cnt_ref.shape)


def router(logits, *, tb=512):
    t_len = logits.shape[0]
    return pl.pallas_call(
        functools.partial(_router_kernel, tb=tb),
        out_shape=(jax.ShapeDtypeStruct((t_len, LANES), F32), jax.ShapeDtypeStruct((SUBLANES, LANES), F32)),
        grid=(t_len // tb,),
        in_specs=[pl.BlockSpec((tb, LANES), lambda i: (i, 0))],
        out_specs=(pl.BlockSpec((tb, LANES), lambda i: (i, 0)), pl.BlockSpec((SUBLANES, LANES), lambda i: (0, 0))),
        scratch_shapes=[pltpu.VMEM((1, LANES), F32)],
        compiler_params=_params(1),
        name="router",
    )(logits)


def _dispatch_kernel(dest_ref, h_ref, xs_in_ref, xs_ref, sem, *, tb):
    del xs_in_ref

    def copy(r, k):
        slot = dest_ref[0, 0, r * TOP_K + k]
        return pltpu.make_async_copy(h_ref.at[pl.ds(r, 1), :], xs_ref.at[pl.ds(slot, 1), :], sem)

    def start(r, c):
        for k in range(TOP_K):
            copy(r, k).start()
        return c

    def wait(r, c):
        for k in range(TOP_K):
            copy(r, k).wait()
        return c

    lax.fori_loop(0, tb, start, 0)
    lax.fori_loop(0, tb, wait, 0)


def dispatch(h, dest, n_rows, *, tb=256):
    t_len, d = h.shape
    dest3 = dest.reshape(t_len // tb, 1, tb * TOP_K)
    xs0 = jnp.zeros((n_rows, d), F32)
    return pl.pallas_call(
        functools.partial(_dispatch_kernel, tb=tb),
        out_shape=jax.ShapeDtypeStruct((n_rows, d), F32),
        grid=(t_len // tb,),
        in_specs=[pl.BlockSpec((1, 1, tb * TOP_K), lambda i: (i, 0, 0), memory_space=pltpu.SMEM),
                  pl.BlockSpec((tb, d), lambda i: (i, 0)),
                  pl.BlockSpec(memory_space=pl.ANY)],
        out_specs=pl.BlockSpec(memory_space=pl.ANY),
        scratch_shapes=[pltpu.SemaphoreType.DMA(())],
        input_output_aliases={2: 0},
        compiler_params=_params(1),
        name="dispatch",
    )(dest3, h, xs0)


def _tile_idx(i, te_ref, na_ref):
    return jnp.minimum(i, na_ref[0] - 1)


def _zero_inactive(na_ref, o_ref):
    @pl.when(pl.program_id(0) >= na_ref[0])
    def _():
        o_ref[...] = jnp.zeros_like(o_ref)


def _gateup_kernel(te_ref, na_ref, x_ref, wg_ref, wu_ref, bg_ref, bu_ref, o_ref, *, tn):
    _zero_inactive(na_ref, o_ref)

    @pl.when(pl.program_id(0) < na_ref[0])
    def _():
        x = x_ref[...].astype(BF16)
        for n in range(0, wg_ref.shape[1], tn):
            g = jnp.dot(x, wg_ref[:, n:n + tn], preferred_element_type=F32) + bg_ref[:, n:n + tn]
            u = jnp.dot(x, wu_ref[:, n:n + tn], preferred_element_type=F32) + bu_ref[:, n:n + tn]
            gate = jnp.minimum(g, SWIGLU_LIMIT)
            up = jnp.clip(u, -SWIGLU_LIMIT, SWIGLU_LIMIT)
            act = (up + 1.0) * gate * jax.nn.sigmoid(SWIGLU_ALPHA * gate)
            o_ref[:, n:n + tn] = act.astype(o_ref.dtype)


def _down_kernel(te_ref, na_ref, x_ref, w_ref, b_ref, o_ref, *, tn):
    _zero_inactive(na_ref, o_ref)

    @pl.when(pl.program_id(0) < na_ref[0])
    def _():
        x = x_ref[...]
        for n in range(0, w_ref.shape[1], tn):
            o_ref[:, n:n + tn] = jnp.dot(x, w_ref[:, n:n + tn], preferred_element_type=F32) + b_ref[:, n:n + tn]


def _grouped_call(kernel, name, tile_e, n_active, x, weights, biases, out_dtype, *, tm, tn):
    n_rows, k = x.shape
    n_out = weights[0].shape[2]
    n_tiles = n_rows // tm
    row_spec = lambda cols: pl.BlockSpec((tm, cols), lambda i, te, na: (_tile_idx(i, te, na), 0))
    w_spec = lambda w: pl.BlockSpec((None,) + w.shape[1:], lambda i, te, na: (te[_tile_idx(i, te, na)], 0, 0))
    return pl.pallas_call(
        functools.partial(kernel, tn=tn),
        out_shape=jax.ShapeDtypeStruct((n_rows, n_out), out_dtype),
        grid_spec=pltpu.PrefetchScalarGridSpec(
            num_scalar_prefetch=2,
            grid=(n_tiles,),
            in_specs=[row_spec(k)] + [w_spec(w) for w in weights] + [w_spec(b) for b in biases],
            out_specs=pl.BlockSpec((tm, n_out), lambda i, te, na: (i, 0))),
        compiler_params=_params(1),
        name=name,
    )(tile_e, n_active, x, *weights, *biases)


def _combine_kernel(dest_ref, gate_ref, h_ref, g_ref, b_ref, y_ref, o_ref, buf_ref, sem, *, tb, alpha):
    def copy(r, k):
        slot = dest_ref[0, 0, r * TOP_K + k]
        return pltpu.make_async_copy(y_ref.at[pl.ds(slot, 1), :], buf_ref.at[k, pl.ds(r, 1), :], sem)

    def start(r, c):
        for k in range(TOP_K):
            copy(r, k).start()
        return c

    def wait(r, c):
        for k in range(TOP_K):
            copy(r, k).wait()
        return c

    lax.fori_loop(0, tb, start, 0)
    lax.fori_loop(0, tb, wait, 0)
    gates = gate_ref[...]
    ffn = buf_ref[0] * gates[:, TOP_K:TOP_K + 1]
    for k in range(1, TOP_K):
        ffn = ffn + buf_ref[k] * gates[:, TOP_K + k:TOP_K + k + 1]
    o_ref[...] = _layer_norm(alpha * h_ref[...] + ffn, g_ref[...], b_ref[...])


def combine(y, dest, route, h, ln_g, ln_b, *, alpha, tb=256):
    t_len, d = h.shape
    dest3 = dest.reshape(t_len // tb, 1, tb * TOP_K)
    return pl.pallas_call(
        functools.partial(_combine_kernel, tb=tb, alpha=alpha),
        out_shape=jax.ShapeDtypeStruct((t_len, d), F32),
        grid=(t_len // tb,),
        in_specs=[pl.BlockSpec((1, 1, tb * TOP_K), lambda i: (i, 0, 0), memory_space=pltpu.SMEM),
                  pl.BlockSpec((tb, LANES), lambda i: (i, 0)),
                  pl.BlockSpec((tb, d), lambda i: (i, 0)),
                  pl.BlockSpec((1, d), lambda i: (0, 0)),
                  pl.BlockSpec((1, d), lambda i: (0, 0)),
                  pl.BlockSpec(memory_space=pl.ANY)],
        out_specs=pl.BlockSpec((tb, d), lambda i: (i, 0)),
        scratch_shapes=[pltpu.VMEM((TOP_K, tb, d), F32), pltpu.SemaphoreType.DMA(())],
        compiler_params=_params(1),
        name="combine",
    )(dest3, route, h, ln_g.reshape(1, d), ln_b.reshape(1, d), y)


def moe(h1, logits, w_gate_up, b_gate_up, w_down, b_down, ln_g, ln_b, *, alpha, tm=512):
    t_len, d = h1.shape
    n_e = w_gate_up.shape[0]
    route, cnt = router(logits)
    eid = route[:, 0:TOP_K].astype(I32)
    pos = route[:, 2 * TOP_K:3 * TOP_K].astype(I32)
    counts = cnt[0, :n_e].astype(I32)
    padded = (counts + tm - 1) // tm * tm
    pend = jnp.cumsum(padded)
    pstart = pend - padded
    onehot = eid[:, :, None] == jnp.arange(n_e, dtype=I32)[None, None, :]
    dest = jnp.sum(jnp.where(onehot, pstart[None, None, :], 0), axis=-1) + pos
    n_tiles = t_len * TOP_K // tm + n_e
    tile_e = jnp.minimum(jnp.searchsorted(pend, jnp.arange(n_tiles, dtype=I32) * tm, side="right"),
                         n_e - 1).astype(I32)
    n_active = (pend[-1:] // tm).astype(I32)

    xs = dispatch(h1, dest, n_tiles * tm)
    wg = w_gate_up[:, :, 0::2].astype(BF16)
    wu = w_gate_up[:, :, 1::2].astype(BF16)
    bg = b_gate_up[:, None, 0::2]
    bu = b_gate_up[:, None, 1::2]
    act = _grouped_call(_gateup_kernel, "moe_gateup", tile_e, n_active, xs, (wg, wu), (bg, bu), BF16, tm=tm, tn=256)
    y = _grouped_call(_down_kernel, "moe_down", tile_e, n_active, act, (w_down.astype(BF16),),
                      (b_down[:, None, :],), F32, tm=tm, tn=256)
    return combine(y, dest, route, h1, ln_g, ln_b, alpha=alpha)


def kernel(x, mem, a_w_in, a_conv_w, a_conv_b, a_rg_w, a_rg_b, a_ig_w, a_ig_b, a_lambda, a_w_out, b_w_q, b_w_out, w_kv_shared, mem_w_kv, ln1_g, ln1_b, ln2_g, ln2_b, router_w, router_b, w_gate_up, b_gate_up, w_down, b_down):
    bsz, seq, d = x.shape
    depth = ln1_g.shape[0]
    n_a = a_w_in.shape[0]
    alpha = float((2 * depth) ** 0.25)
    outs = []
    for bi in range(bsz):
        h = x[bi]
        mem_b = mem[bi]
        qkv = None
        for layer in range(depth):
            mem_kv = dense(mem_b, mem_w_kv[layer].astype(BF16), BF16, tm=mem_b.shape[0], tnb=2 * MEM_WIDTH)
            if layer < n_a:
                lw = a_rg_w.shape[1] * LANES
                proj = dense(h, a_w_in[layer].astype(BF16), F32, tm=512, tnb=(2 * lw + MEM_WIDTH) // 2)
                main = rglru(proj, a_conv_w[layer], a_conv_b[layer], a_rg_w[layer], a_rg_b[layer],
                             a_ig_w[layer], a_ig_b[layer], a_lambda[layer])
                qsrc, q_col, w_out = proj, 2 * lw // MEM_WIDTH, a_w_out[layer]
            else:
                j = layer - n_a
                sbw = w_kv_shared.shape[1] // 2
                n_heads = sbw // HEAD_DIM
                if j == 0:
                    w_cat = jnp.concatenate([w_kv_shared, b_w_q[j]], axis=1).astype(BF16)
                    qkv = dense(h, w_cat, BF16, tm=512, tnb=w_cat.shape[1] // 4)
                else:
                    q_only = dense(h, b_w_q[j].astype(BF16), BF16, tm=512, tnb=b_w_q[j].shape[1] // 2)
                    qkv = jnp.concatenate([qkv[:, :2 * sbw], q_only], axis=1)
                main = stick_breaking(qkv, n_heads=n_heads, q_col=2 * n_heads, k_col=0, v_col=n_heads)
                qsrc, q_col, w_out = qkv, (3 * sbw) // MEM_WIDTH, b_w_out[j]
            h1, logits = mixout(main, qsrc, q_col, mem_kv, w_out, h, ln1_g[layer], ln1_b[layer],
                                router_w[layer], router_b[layer], alpha=alpha)
            h = moe(h1, logits, w_gate_up[layer], b_gate_up[layer], w_down[layer], b_down[layer],
                    ln2_g[layer], ln2_b[layer], alpha=alpha)
        outs.append(h)
    return jnp.stack(outs, axis=0)
```

```python
import functools

import jax
import jax.numpy as jnp
from jax import lax
from jax.experimental import pallas as pl
from jax.experimental.pallas import tpu as pltpu

F32 = jnp.float32
BF16 = jnp.bfloat16
I32 = jnp.int32

HEAD_DIM = 128
MEM_HEADS = 4
MEM_WIDTH = MEM_HEADS * HEAD_DIM
CONV_WIDTH = 4
LRU_C = 8.0
N_EXPERTS = 32
TOP_K = 4
SWIGLU_LIMIT = 7.0
SWIGLU_ALPHA = 1.702
LN_EPS = 1e-5
LANES = 128
SUBLANES = 8
GU = 2 * LANES
VMEM_LIMIT = 56 * 1024 * 1024
SB_SKIP = 100.0


def _params(n_axes, vmem=VMEM_LIMIT):
    return pltpu.CompilerParams(dimension_semantics=("arbitrary",) * n_axes, vmem_limit_bytes=vmem)


def _dense_kernel(x_ref, w_ref, o_ref, xb_ref, *, tn):
    @pl.when(pl.program_id(1) == 0)
    def _():
        xb_ref[...] = x_ref[...].astype(BF16)

    xb = xb_ref[...]
    for n in range(0, w_ref.shape[1], tn):
        o_ref[:, n:n + tn] = jnp.dot(xb, w_ref[:, n:n + tn], preferred_element_type=F32).astype(o_ref.dtype)


def dense(x, w, out_dtype, *, tm, tnb, tn=256):
    m, k = x.shape
    n = w.shape[1]
    return pl.pallas_call(
        functools.partial(_dense_kernel, tn=tn),
        out_shape=jax.ShapeDtypeStruct((m, n), out_dtype),
        grid=(m // tm, n // tnb),
        in_specs=[pl.BlockSpec((tm, k), lambda i, j: (i, 0)),
                  pl.BlockSpec((k, tnb), lambda i, j: (0, j))],
        out_specs=pl.BlockSpec((tm, tnb), lambda i, j: (i, j)),
        scratch_shapes=[pltpu.VMEM((tm, k), BF16)],
        compiler_params=_params(2),
        name="dense",
    )(x, w)


def _softplus(x):
    return jnp.maximum(x, 0.0) + jnp.log1p(jnp.exp(-jnp.abs(x)))


def _gelu_tanh(x):
    return 0.5 * x * (1.0 + jnp.tanh(0.7978845608028654 * (x + 0.044715 * (x * x * x))))


def _rglru_kernel(gate_ref, u_ref, cw_ref, cb_ref, rgw_ref, rgb_ref, igw_ref, igb_ref, lam_ref,
                  o_ref, hist_ref, carry_ref, *, tt):
    t = pl.program_id(1)

    @pl.when(t == 0)
    def _():
        hist_ref[...] = jnp.zeros_like(hist_ref)
        carry_ref[...] = jnp.zeros_like(carry_ref)

    u = u_ref[...]
    ext = jnp.concatenate([hist_ref[...], u], axis=0)
    cw = cw_ref[...]
    xc = cb_ref[...] + u * cw[CONV_WIDTH - 1:CONV_WIDTH, :]
    for d in range(1, CONV_WIDTH):
        shifted = pltpu.roll(ext, d, axis=0)[SUBLANES:, :]
        xc = xc + shifted * cw[CONV_WIDTH - 1 - d:CONV_WIDTH - d, :]
    hist_ref[...] = u[tt - SUBLANES:, :]

    xcb = xc.astype(BF16)
    r = jax.nn.sigmoid(jnp.dot(xcb, rgw_ref[...], preferred_element_type=F32) + rgb_ref[...])
    ig = jax.nn.sigmoid(jnp.dot(xcb, igw_ref[...], preferred_element_type=F32) + igb_ref[...])
    log_a = (-LRU_C) * r * _softplus(-lam_ref[...])
    a = jnp.exp(log_a)
    b = jnp.sqrt(-jnp.tanh(log_a) * (a * a + 1.0)) * (ig * xc)

    row = lax.broadcasted_iota(I32, (tt, LANES), 0)
    d = 1
    while d < tt:
        keep = row >= d
        a_sh = jnp.where(keep, pltpu.roll(a, d, axis=0), 1.0)
        b_sh = jnp.where(keep, pltpu.roll(b, d, axis=0), 0.0)
        b = a * b_sh + b
        a = a * a_sh
        d *= 2
    h = a * carry_ref[...] + b
    carry_ref[...] = h[tt - 1:tt, :]
    o_ref[...] = (h * _gelu_tanh(gate_ref[...])).astype(o_ref.dtype)


def rglru(proj, conv_w, conv_b, rg_w, rg_b, ig_w, ig_b, lam, *, tt=512):
    t_len = proj.shape[0]
    nblk = rg_w.shape[0]
    width = nblk * LANES
    vec = lambda v: v.reshape(1, width)
    chan = lambda rows: pl.BlockSpec((rows, LANES), lambda n, t: (0, n))
    return pl.pallas_call(
        functools.partial(_rglru_kernel, tt=tt),
        out_shape=jax.ShapeDtypeStruct((t_len, width), BF16),
        grid=(nblk, t_len // tt),
        in_specs=[pl.BlockSpec((tt, LANES), lambda n, t: (t, n)),
                  pl.BlockSpec((tt, LANES), lambda n, t: (t, nblk + n)),
                  chan(CONV_WIDTH), chan(1),
                  pl.BlockSpec((None, LANES, LANES), lambda n, t: (n, 0, 0)), chan(1),
                  pl.BlockSpec((None, LANES, LANES), lambda n, t: (n, 0, 0)), chan(1),
                  chan(1)],
        out_specs=pl.BlockSpec((tt, LANES), lambda n, t: (t, n)),
        scratch_shapes=[pltpu.VMEM((SUBLANES, LANES), F32), pltpu.VMEM((1, LANES), F32)],
        compiler_params=_params(2),
        name="rglru",
    )(proj, proj, conv_w, vec(conv_b), rg_w.astype(BF16), vec(rg_b), ig_w.astype(BF16), vec(ig_b), vec(lam))


def _sb_kernel(q_ref, k_ref, v_ref, o_ref, acc_ref, run_ref, *, tq):
    i = pl.program_id(1)
    q = q_ref[...]
    scale = HEAD_DIM ** -0.5
    rows = lax.broadcasted_iota(I32, (tq, tq), 0)
    cols = lax.broadcasted_iota(I32, (tq, tq), 1)
    later = (rows > cols).astype(BF16)
    causal = cols < rows

    def rows_of(j):
        return pl.ds(j * tq if isinstance(j, int) else pl.multiple_of(j * tq, tq), tq)

    def front(j, masked):
        z = lax.dot_general(q, k_ref[rows_of(j), :], (((1,), (1,)), ((), ())), preferred_element_type=F32) * scale
        sp = _softplus(z)
        log_keep = -sp
        log_beta = z - sp
        if masked:
            log_keep = jnp.where(causal, log_keep, 0.0)
        hi = log_keep.astype(BF16)
        lo = (log_keep - hi.astype(F32)).astype(BF16)
        after = (jnp.dot(hi, later, preferred_element_type=F32)
                 + jnp.dot(lo, later, preferred_element_type=F32))
        return log_beta + after, after[:, 0:1] + log_keep[:, 0:1]

    def back(j, logw, run, masked):
        w = jnp.exp(logw if run is None else logw + run)
        if masked:
            w = jnp.where(causal, w, 0.0)
        return jnp.dot(w.astype(BF16), v_ref[rows_of(j), :], preferred_element_type=F32)

    @pl.when(i == 0)
    def _():
        logw, _ = front(0, True)
        o_ref[...] = back(0, logw, None, True).astype(o_ref.dtype)

    @pl.when(i > 0)
    def _():
        logw_d, tot_d = front(i, True)
        logw_p, tot_p = front(i - 1, False)
        acc_ref[...] = back(i, logw_d, None, True) + back(i - 1, logw_p, tot_d, False)
        run0 = tot_d + tot_p
        run_ref[...] = run0

        def cond(state):
            j, top = state
            return jnp.logical_and(j >= 0, top > -SB_SKIP)

        def body(state):
            j, _ = state
            logw, tot = front(j, False)
            run = run_ref[...]
            acc_ref[...] += back(j, logw, run, False)
            run = run + tot
            run_ref[...] = run
            return j - 1, jnp.max(run)

        lax.while_loop(cond, body, (i - 2, jnp.max(run0)))
        o_ref[...] = acc_ref[...].astype(o_ref.dtype)


def stick_breaking(qkv, *, n_heads, q_col, k_col, v_col, tq=256):
    t_len = qkv.shape[0]
    return pl.pallas_call(
        functools.partial(_sb_kernel, tq=tq),
        out_shape=jax.ShapeDtypeStruct((t_len, n_heads * HEAD_DIM), BF16),
        grid=(n_heads, t_len // tq),
        in_specs=[pl.BlockSpec((tq, HEAD_DIM), lambda h, i: (i, q_col + h)),
                  pl.BlockSpec((t_len, HEAD_DIM), lambda h, i: (0, k_col + h)),
                  pl.BlockSpec((t_len, HEAD_DIM), lambda h, i: (0, v_col + h))],
        out_specs=pl.BlockSpec((tq, HEAD_DIM), lambda h, i: (i, h)),
        scratch_shapes=[pltpu.VMEM((tq, HEAD_DIM), F32), pltpu.VMEM((tq, 1), F32)],
        compiler_params=_params(2),
        name="stick_breaking",
    )(qkv, qkv, qkv)


def _layer_norm(y, g, b):
    mu = jnp.mean(y, axis=-1, keepdims=True)
    yc = y - mu
    var = jnp.mean(yc * yc, axis=-1, keepdims=True)
    return yc * lax.rsqrt(var + LN_EPS) * g + b


def _mixout_kernel(main_ref, qm_ref, mkv_ref, wout_ref, h_ref, g_ref, b_ref, rwh_ref, rwl_ref, rb_ref,
                   o_ref, logit_ref, *, alpha):
    main_w = main_ref.shape[1]
    mix = jnp.dot(main_ref[...], wout_ref[0:main_w, :], preferred_element_type=F32)
    q = qm_ref[...].astype(BF16)
    heads = []
    for hd in range(MEM_HEADS):
        lo, hi = hd * HEAD_DIM, (hd + 1) * HEAD_DIM
        kh = mkv_ref[:, lo:hi]
        vh = mkv_ref[:, MEM_WIDTH + lo:MEM_WIDTH + hi]
        s = lax.dot_general(q[:, lo:hi], kh, (((1,), (1,)), ((), ())),
                            preferred_element_type=F32) * (HEAD_DIM ** -0.5)
        e = jnp.exp(s - jnp.max(s, axis=-1, keepdims=True))
        p = e * (1.0 / jnp.sum(e, axis=-1, keepdims=True))
        heads.append(jnp.dot(p.astype(BF16), vh, preferred_element_type=F32).astype(BF16))
    mem_out = jnp.concatenate(heads, axis=-1)
    mix = mix + jnp.dot(mem_out, wout_ref[main_w:, :], preferred_element_type=F32)
    h1 = _layer_norm(alpha * h_ref[...] + mix, g_ref[...], b_ref[...])
    o_ref[...] = h1
    hh = h1.astype(BF16)
    hl = (h1 - hh.astype(F32)).astype(BF16)
    logit_ref[...] = (jnp.dot(hh, rwh_ref[...], preferred_element_type=F32)
                      + jnp.dot(hl, rwh_ref[...], preferred_element_type=F32)
                      + jnp.dot(hh, rwl_ref[...], preferred_element_type=F32) + rb_ref[...])


def mixout(main, qsrc, q_col, mem_kv, w_out, h, ln_g, ln_b, router_w, router_b, *, alpha, tm=256):
    t_len, d = h.shape
    main_w = main.shape[1]
    n_e = router_w.shape[1]
    rw = jnp.zeros((d, LANES), F32).at[:, :n_e].set(router_w)
    rwh = rw.astype(BF16)
    rwl = (rw - rwh.astype(F32)).astype(BF16)
    rb = jnp.full((1, LANES), -jnp.inf, F32).at[0, :n_e].set(router_b)
    full = lambda a: pl.BlockSpec(a.shape, lambda i: (0, 0))
    args = (main, qsrc, mem_kv, w_out.astype(BF16), h, ln_g.reshape(1, d), ln_b.reshape(1, d), rwh, rwl, rb)
    return pl.pallas_call(
        functools.partial(_mixout_kernel, alpha=alpha),
        out_shape=(jax.ShapeDtypeStruct((t_len, d), F32), jax.ShapeDtypeStruct((t_len, LANES), F32)),
        grid=(t_len // tm,),
        in_specs=[pl.BlockSpec((tm, main_w), lambda i: (i, 0)),
                  pl.BlockSpec((tm, MEM_WIDTH), lambda i: (i, q_col)),
                  full(mem_kv), full(args[3]),
                  pl.BlockSpec((tm, d), lambda i: (i, 0)),
                  full(args[5]), full(args[6]), full(rwh), full(rwl), full(rb)],
        out_specs=(pl.BlockSpec((tm, d), lambda i: (i, 0)), pl.BlockSpec((tm, LANES), lambda i: (i, 0))),
        compiler_params=_params(1),
        name="mixout",
    )(*args)


def _router_kernel(logit_ref, route_ref, cnt_ref, carry_ref, *, tb):
    @pl.when(pl.program_id(0) == 0)
    def _():
        carry_ref[...] = jnp.zeros_like(carry_ref)

    vals = logit_ref[...]
    lane = lax.broadcasted_iota(I32, (tb, LANES), 1).astype(F32)
    sels, tops, ids = [], [], []
    for _ in range(TOP_K):
        m = jnp.max(vals, axis=-1, keepdims=True)
        idx = jnp.min(jnp.where(vals == m, lane, float(LANES)), axis=-1, keepdims=True)
        sel = lane == idx
        vals = jnp.where(sel, -jnp.inf, vals)
        sels.append(sel)
        tops.append(m)
        ids.append(idx)
    exps = [jnp.exp(m - tops[0]) for m in tops]
    inv = 1.0 / (exps[0] + exps[1] + exps[2] + exps[3])
    onehot = jnp.zeros((tb, LANES), F32)
    for sel in sels:
        onehot = onehot + sel.astype(F32)
    rows = lax.broadcasted_iota(I32, (tb, tb), 0)
    cols = lax.broadcasted_iota(I32, (tb, tb), 1)
    before = (cols < rows).astype(BF16)
    cum = jnp.dot(before, onehot.astype(BF16), preferred_element_type=F32) + carry_ref[...]
    out = jnp.zeros((tb, LANES), F32)
    for k in range(TOP_K):
        pos = jnp.sum(jnp.where(sels[k], cum, 0.0), axis=-1, keepdims=True)
        out = jnp.where(lane == k, ids[k], out)
        out = jnp.where(lane == TOP_K + k, exps[k] * inv, out)
        out = jnp.where(lane == 2 * TOP_K + k, pos, out)
    route_ref[...] = out
    carry_ref[...] += jnp.sum(onehot, axis=0, keepdims=True)
    cnt_ref[...] = jnp.broadcast_to(carry_ref[...], cnt_ref.shape)


def router(logits, *, tb=512):
    t_len = logits.shape[0]
    return pl.pallas_call(
        functools.partial(_router_kernel, tb=tb),
        out_shape=(jax.ShapeDtypeStruct((t_len, LANES), F32), jax.ShapeDtypeStruct((SUBLANES, LANES), F32)),
        grid=(t_len // tb,),
        in_specs=[pl.BlockSpec((tb, LANES), lambda i: (i, 0))],
        out_specs=(pl.BlockSpec((tb, LANES), lambda i: (i, 0)), pl.BlockSpec((SUBLANES, LANES), lambda i: (0, 0))),
        scratch_shapes=[pltpu.VMEM((1, LANES), F32)],
        compiler_params=_params(1),
        name="router",
    )(logits)


def _dispatch_kernel(dest_ref, h_ref, xs_in_ref, xs_ref, sem, *, tb):
    del xs_in_ref

    def copy(r, k):
        slot = dest_ref[0, 0, r * TOP_K + k]
        return pltpu.make_async_copy(h_ref.at[pl.ds(r, 1), :], xs_ref.at[pl.ds(slot, 1), :], sem)

    def start(r, c):
        for k in range(TOP_K):
            copy(r, k).start()
        return c

    def wait(r, c):
        for k in range(TOP_K):
            copy(r, k).wait()
        return c

    lax.fori_loop(0, tb, start, 0)
    lax.fori_loop(0, tb, wait, 0)


def dispatch(h, dest, n_rows, *, tb=256):
    t_len, d = h.shape
    dest3 = dest.reshape(t_len // tb, 1, tb * TOP_K)
    xs0 = jnp.zeros((n_rows, d), F32)
    return pl.pallas_call(
        functools.partial(_dispatch_kernel, tb=tb),
        out_shape=jax.ShapeDtypeStruct((n_rows, d), F32),
        grid=(t_len // tb,),
        in_specs=[pl.BlockSpec((1, 1, tb * TOP_K), lambda i: (i, 0, 0), memory_space=pltpu.SMEM),
                  pl.BlockSpec((tb, d), lambda i: (i, 0)),
                  pl.BlockSpec(memory_space=pl.ANY)],
        out_specs=pl.BlockSpec(memory_space=pl.ANY),
        scratch_shapes=[pltpu.SemaphoreType.DMA(())],
        input_output_aliases={2: 0},
        compiler_params=_params(1),
        name="dispatch",
    )(dest3, h, xs0)


def _tile_idx(i, te_ref, na_ref):
    return jnp.minimum(i, na_ref[0] - 1)


def _zero_inactive(i, na_ref, o_ref):
    @pl.when(i >= na_ref[0])
    def _():
        o_ref[...] = jnp.zeros_like(o_ref)


def _expert_changed(i, te_ref):
    return jnp.logical_or(i == 0, te_ref[i] != te_ref[jnp.maximum(i - 1, 0)])


def _gateup_kernel(te_ref, na_ref, x_ref, w_ref, b_ref, o_ref, wb_ref):
    i = pl.program_id(1)
    nh = w_ref.shape[1]
    _zero_inactive(i, na_ref, o_ref)

    @pl.when(i < na_ref[0])
    def _():
        @pl.when(_expert_changed(i, te_ref))
        def _():
            rows = lax.broadcasted_iota(I32, (GU, GU), 0)
            cols = lax.broadcasted_iota(I32, (GU, GU), 1)
            src = jnp.where(cols < LANES, 2 * cols, 2 * (cols - LANES) + 1)
            perm = (rows == src).astype(BF16)
            for n in range(0, nh, GU):
                wb_ref[:, n:n + GU] = jnp.dot(w_ref[:, n:n + GU].astype(BF16), perm,
                                              preferred_element_type=F32).astype(BF16)

        x = x_ref[...].astype(BF16)
        for n in range(0, nh, GU):
            gu = jnp.dot(x, wb_ref[:, n:n + GU], preferred_element_type=F32) + b_ref[:, n:n + GU]
            gate = jnp.minimum(gu[:, :LANES], SWIGLU_LIMIT)
            up = jnp.clip(gu[:, LANES:], -SWIGLU_LIMIT, SWIGLU_LIMIT)
            act = (up + 1.0) * gate * jax.nn.sigmoid(SWIGLU_ALPHA * gate)
            o_ref[:, n // 2:n // 2 + LANES] = act.astype(o_ref.dtype)


def moe_gateup(tile_e, n_active, x, w, b, *, tm):
    n_rows, k = x.shape
    n2 = w.shape[2]
    nh = n2 // 2
    return pl.pallas_call(
        _gateup_kernel,
        out_shape=jax.ShapeDtypeStruct((n_rows, n2 // 2), BF16),
        grid_spec=pltpu.PrefetchScalarGridSpec(
            num_scalar_prefetch=2,
            grid=(2, n_rows // tm),
            in_specs=[pl.BlockSpec((tm, k), lambda hf, i, te, na: (_tile_idx(i, te, na), 0)),
                      pl.BlockSpec((None, k, nh), lambda hf, i, te, na: (te[_tile_idx(i, te, na)], 0, hf)),
                      pl.BlockSpec((None, 1, nh), lambda hf, i, te, na: (te[_tile_idx(i, te, na)], 0, hf))],
            out_specs=pl.BlockSpec((tm, nh // 2), lambda hf, i, te, na: (i, hf)),
            scratch_shapes=[pltpu.VMEM((k, nh), BF16)]),
        compiler_params=_params(2),
        name="moe_gateup",
    )(tile_e, n_active, x, w, b)


def _down_kernel(te_ref, na_ref, x_ref, w_ref, b_ref, o_ref, wb_ref, *, tn):
    i = pl.program_id(0)
    _zero_inactive(i, na_ref, o_ref)

    @pl.when(i < na_ref[0])
    def _():
        @pl.when(_expert_changed(i, te_ref))
        def _():
            wb_ref[...] = w_ref[...].astype(BF16)

        x = x_ref[...]
        for n in range(0, w_ref.shape[1], tn):
            o_ref[:, n:n + tn] = jnp.dot(x, wb_ref[:, n:n + tn], preferred_element_type=F32) + b_ref[:, n:n + tn]


def moe_down(tile_e, n_active, x, w, b, *, tm, tn=256):
    n_rows, k = x.shape
    n_out = w.shape[2]
    return pl.pallas_call(
        functools.partial(_down_kernel, tn=tn),
        out_shape=jax.ShapeDtypeStruct((n_rows, n_out), F32),
        grid_spec=pltpu.PrefetchScalarGridSpec(
            num_scalar_prefetch=2,
            grid=(n_rows // tm,),
            in_specs=[pl.BlockSpec((tm, k), lambda i, te, na: (_tile_idx(i, te, na), 0)),
                      pl.BlockSpec((None, k, n_out), lambda i, te, na: (te[_tile_idx(i, te, na)], 0, 0)),
                      pl.BlockSpec((None, 1, n_out), lambda i, te, na: (te[_tile_idx(i, te, na)], 0, 0))],
            out_specs=pl.BlockSpec((tm, n_out), lambda i, te, na: (i, 0)),
            scratch_shapes=[pltpu.VMEM((k, n_out), BF16)]),
        compiler_params=_params(1),
        name="moe_down",
    )(tile_e, n_active, x, w, b)


def _combine_kernel(dest_ref, gate_ref, h_ref, g_ref, b_ref, y_ref, o_ref, buf_ref, sem, *, tb, alpha):
    def copy(r, k):
        slot = dest_ref[0, 0, r * TOP_K + k]
        return pltpu.make_async_copy(y_ref.at[pl.ds(slot, 1), :], buf_ref.at[k, pl.ds(r, 1), :], sem)

    def start(r, c):
        for k in range(TOP_K):
            copy(r, k).start()
        return c

    def wait(r, c):
        for k in range(TOP_K):
            copy(r, k).wait()
        return c

    lax.fori_loop(0, tb, start, 0)
    lax.fori_loop(0, tb, wait, 0)
    gates = gate_ref[...]
    ffn = buf_ref[0] * gates[:, TOP_K:TOP_K + 1]
    for k in range(1, TOP_K):
        ffn = ffn + buf_ref[k] * gates[:, TOP_K + k:TOP_K + k + 1]
    o_ref[...] = _layer_norm(alpha * h_ref[...] + ffn, g_ref[...], b_ref[...])


def combine(y, dest, route, h, ln_g, ln_b, *, alpha, tb=256):
    t_len, d = h.shape
    dest3 = dest.reshape(t_len // tb, 1, tb * TOP_K)
    return pl.pallas_call(
        functools.partial(_combine_kernel, tb=tb, alpha=alpha),
        out_shape=jax.ShapeDtypeStruct((t_len, d), F32),
        grid=(t_len // tb,),
        in_specs=[pl.BlockSpec((1, 1, tb * TOP_K), lambda i: (i, 0, 0), memory_space=pltpu.SMEM),
                  pl.BlockSpec((tb, LANES), lambda i: (i, 0)),
                  pl.BlockSpec((tb, d), lambda i: (i, 0)),
                  pl.BlockSpec((1, d), lambda i: (0, 0)),
                  pl.BlockSpec((1, d), lambda i: (0, 0)),
                  pl.BlockSpec(memory_space=pl.ANY)],
        out_specs=pl.BlockSpec((tb, d), lambda i: (i, 0)),
        scratch_shapes=[pltpu.VMEM((TOP_K, tb, d), F32), pltpu.SemaphoreType.DMA(())],
        compiler_params=_params(1),
        name="combine",
    )(dest3, route, h, ln_g.reshape(1, d), ln_b.reshape(1, d), y)


def moe(h1, logits, w_gate_up, b_gate_up, w_down, b_down, ln_g, ln_b, *, alpha, tm=512):
    t_len, d = h1.shape
    n_e = w_gate_up.shape[0]
    route, cnt = router(logits)
    eid = route[:, 0:TOP_K].astype(I32)
    pos = route[:, 2 * TOP_K:3 * TOP_K].astype(I32)
    counts = cnt[0, :n_e].astype(I32)
    padded = (counts + tm - 1) // tm * tm
    pend = jnp.cumsum(padded)
    pstart = pend - padded
    onehot = eid[:, :, None] == jnp.arange(n_e, dtype=I32)[None, None, :]
    dest = jnp.sum(jnp.where(onehot, pstart[None, None, :], 0), axis=-1) + pos
    n_tiles = t_len * TOP_K // tm + n_e
    tile_start = jnp.arange(n_tiles, dtype=I32) * tm
    tile_e = jnp.minimum(jnp.sum((pend[None, :] <= tile_start[:, None]).astype(I32), axis=1), n_e - 1)
    n_active = (pend[-1:] // tm).astype(I32)

    xs = dispatch(h1, dest, n_tiles * tm)
    bgu = b_gate_up.reshape(n_e, -1, LANES, 2).transpose(0, 1, 3, 2).reshape(n_e, 1, -1)
    act = moe_gateup(tile_e, n_active, xs, w_gate_up, bgu, tm=tm)
    y = moe_down(tile_e, n_active, act, w_down, b_down[:, None, :], tm=tm)
    return combine(y, dest, route, h1, ln_g, ln_b, alpha=alpha)


def kernel(x, mem, a_w_in, a_conv_w, a_conv_b, a_rg_w, a_rg_b, a_ig_w, a_ig_b, a_lambda, a_w_out, b_w_q, b_w_out, w_kv_shared, mem_w_kv, ln1_g, ln1_b, ln2_g, ln2_b, router_w, router_b, w_gate_up, b_gate_up, w_down, b_down):
    bsz, seq, d = x.shape
    depth = ln1_g.shape[0]
    n_a = a_w_in.shape[0]
    alpha = float((2 * depth) ** 0.25)
    outs = []
    for bi in range(bsz):
        h = x[bi]
        mem_b = mem[bi]
        qkv = None
        for layer in range(depth):
            mem_kv = dense(mem_b, mem_w_kv[layer].astype(BF16), BF16, tm=mem_b.shape[0], tnb=2 * MEM_WIDTH)
            if layer < n_a:
                lw = a_rg_w.shape[1] * LANES
                proj = dense(h, a_w_in[layer].astype(BF16), F32, tm=512, tnb=(2 * lw + MEM_WIDTH) // 2)
                main = rglru(proj, a_conv_w[layer], a_conv_b[layer], a_rg_w[layer], a_rg_b[layer],
                             a_ig_w[layer], a_ig_b[layer], a_lambda[layer])
                qsrc, q_col, w_out = proj, 2 * lw // MEM_WIDTH, a_w_out[layer]
            else:
                j = layer - n_a
                sbw = w_kv_shared.shape[1] // 2
                n_heads = sbw // HEAD_DIM
                if j == 0:
                    w_cat = jnp.concatenate([w_kv_shared, b_w_q[j]], axis=1).astype(BF16)
                    qkv = dense(h, w_cat, BF16, tm=512, tnb=w_cat.shape[1] // 4)
                else:
                    q_only = dense(h, b_w_q[j].astype(BF16), BF16, tm=512, tnb=b_w_q[j].shape[1] // 2)
                    qkv = jnp.concatenate([qkv[:, :2 * sbw], q_only], axis=1)
                main = stick_breaking(qkv, n_heads=n_heads, q_col=2 * n_heads, k_col=0, v_col=n_heads)
                qsrc, q_col, w_out = qkv, (3 * sbw) // MEM_WIDTH, b_w_out[j]
            h1, logits = mixout(main, qsrc, q_col, mem_kv, w_out, h, ln1_g[layer], ln1_b[layer],
                                router_w[layer], router_b[layer], alpha=alpha)
            h = moe(h1, logits, w_gate_up[layer], b_gate_up[layer], w_down[layer], b_down[layer],
                    ln2_g[layer], ln2_b[layer], alpha=alpha)
        outs.append(h)
    return jnp.stack(outs, axis=0)
```

```python
import functools

import jax
import jax.numpy as jnp
from jax import lax
from jax.experimental import pallas as pl
from jax.experimental.pallas import tpu as pltpu

F32 = jnp.float32
BF16 = jnp.bfloat16
I32 = jnp.int32

HEAD_DIM = 128
MEM_HEADS = 4
MEM_WIDTH = MEM_HEADS * HEAD_DIM
CONV_WIDTH = 4
LRU_C = 8.0
N_EXPERTS = 32
TOP_K = 4
SWIGLU_LIMIT = 7.0
SWIGLU_ALPHA = 1.702
LN_EPS = 1e-5
LANES = 128
SUBLANES = 8
GU = 2 * LANES
VMEM_LIMIT = 56 * 1024 * 1024
SB_SKIP = 100.0


def _params(n_axes, vmem=VMEM_LIMIT):
    return pltpu.CompilerParams(dimension_semantics=("arbitrary",) * n_axes, vmem_limit_bytes=vmem)


def _dense_kernel(x_ref, w_ref, o_ref, xb_ref, *, tn):
    @pl.when(pl.program_id(1) == 0)
    def _():
        xb_ref[...] = x_ref[...].astype(BF16)

    xb = xb_ref[...]
    for n in range(0, w_ref.shape[1], tn):
        o_ref[:, n:n + tn] = jnp.dot(xb, w_ref[:, n:n + tn], preferred_element_type=F32).astype(o_ref.dtype)


def dense(x, w, out_dtype, *, tm, tnb, tn=256):
    m, k = x.shape
    n = w.shape[1]
    return pl.pallas_call(
        functools.partial(_dense_kernel, tn=tn),
        out_shape=jax.ShapeDtypeStruct((m, n), out_dtype),
        grid=(m // tm, n // tnb),
        in_specs=[pl.BlockSpec((tm, k), lambda i, j: (i, 0)),
                  pl.BlockSpec((k, tnb), lambda i, j: (0, j))],
        out_specs=pl.BlockSpec((tm, tnb), lambda i, j: (i, j)),
        scratch_shapes=[pltpu.VMEM((tm, k), BF16)],
        compiler_params=_params(2),
        name="dense",
    )(x, w)


def _softplus(x):
    return jnp.maximum(x, 0.0) + jnp.log1p(jnp.exp(-jnp.abs(x)))


def _gelu_tanh(x):
    return 0.5 * x * (1.0 + jnp.tanh(0.7978845608028654 * (x + 0.044715 * (x * x * x))))


def _rglru_kernel(gate_ref, u_ref, cw_ref, cb_ref, rgw_ref, rgb_ref, igw_ref, igb_ref, lam_ref,
                  o_ref, hist_ref, carry_ref, *, tt):
    t = pl.program_id(1)

    @pl.when(t == 0)
    def _():
        hist_ref[...] = jnp.zeros_like(hist_ref)
        carry_ref[...] = jnp.zeros_like(carry_ref)

    u = u_ref[...]
    ext = jnp.concatenate([hist_ref[...], u], axis=0)
    cw = cw_ref[...]
    xc = cb_ref[...] + u * cw[CONV_WIDTH - 1:CONV_WIDTH, :]
    for d in range(1, CONV_WIDTH):
        shifted = pltpu.roll(ext, d, axis=0)[SUBLANES:, :]
        xc = xc + shifted * cw[CONV_WIDTH - 1 - d:CONV_WIDTH - d, :]
    hist_ref[...] = u[tt - SUBLANES:, :]

    xcb = xc.astype(BF16)
    r = jax.nn.sigmoid(jnp.dot(xcb, rgw_ref[...], preferred_element_type=F32) + rgb_ref[...])
    ig = jax.nn.sigmoid(jnp.dot(xcb, igw_ref[...], preferred_element_type=F32) + igb_ref[...])
    log_a = (-LRU_C) * r * _softplus(-lam_ref[...])
    a = jnp.exp(log_a)
    b = jnp.sqrt(-jnp.tanh(log_a) * (a * a + 1.0)) * (ig * xc)

    groups = tt // SUBLANES
    a = a.reshape(groups, SUBLANES, LANES)
    b = b.reshape(groups, SUBLANES, LANES)
    sub = lax.broadcasted_iota(I32, (groups, SUBLANES, LANES), 1)
    d = 1
    while d < SUBLANES:
        keep = sub >= d
        a_sh = jnp.where(keep, pltpu.roll(a, d, axis=1), 1.0)
        b_sh = jnp.where(keep, pltpu.roll(b, d, axis=1), 0.0)
        b = a * b_sh + b
        a = a * a_sh
        d *= 2
    a = a.reshape(tt, LANES)
    b = b.reshape(tt, LANES)
    carry = carry_ref[...]
    hs = []
    for g in range(tt // SUBLANES):
        rows = slice(g * SUBLANES, (g + 1) * SUBLANES)
        hs.append(a[rows, :] * carry + b[rows, :])
        carry = hs[-1][SUBLANES - 1:SUBLANES, :]
    carry_ref[...] = carry
    o_ref[...] = (jnp.concatenate(hs, axis=0) * _gelu_tanh(gate_ref[...])).astype(o_ref.dtype)


def rglru(proj, conv_w, conv_b, rg_w, rg_b, ig_w, ig_b, lam, *, tt=512):
    t_len = proj.shape[0]
    nblk = rg_w.shape[0]
    width = nblk * LANES
    vec = lambda v: v.reshape(1, width)
    chan = lambda rows: pl.BlockSpec((rows, LANES), lambda n, t: (0, n))
    return pl.pallas_call(
        functools.partial(_rglru_kernel, tt=tt),
        out_shape=jax.ShapeDtypeStruct((t_len, width), BF16),
        grid=(nblk, t_len // tt),
        in_specs=[pl.BlockSpec((tt, LANES), lambda n, t: (t, n)),
                  pl.BlockSpec((tt, LANES), lambda n, t: (t, nblk + n)),
                  chan(CONV_WIDTH), chan(1),
                  pl.BlockSpec((None, LANES, LANES), lambda n, t: (n, 0, 0)), chan(1),
                  pl.BlockSpec((None, LANES, LANES), lambda n, t: (n, 0, 0)), chan(1),
                  chan(1)],
        out_specs=pl.BlockSpec((tt, LANES), lambda n, t: (t, n)),
        scratch_shapes=[pltpu.VMEM((SUBLANES, LANES), F32), pltpu.VMEM((1, LANES), F32)],
        compiler_params=_params(2),
        name="rglru",
    )(proj, proj, conv_w, vec(conv_b), rg_w.astype(BF16), vec(rg_b), ig_w.astype(BF16), vec(ig_b), vec(lam))


def _sb_kernel(q_ref, k_ref, v_ref, o_ref, acc_ref, run_ref, *, tq):
    i = pl.program_id(1)
    q = q_ref[...]
    scale = HEAD_DIM ** -0.5
    rows = lax.broadcasted_iota(I32, (tq, tq), 0)
    cols = lax.broadcasted_iota(I32, (tq, tq), 1)
    later = (rows > cols).astype(BF16)
    causal = cols < rows

    def rows_of(j):
        return pl.ds(j * tq if isinstance(j, int) else pl.multiple_of(j * tq, tq), tq)

    def front(j, masked):
        z = lax.dot_general(q, k_ref[rows_of(j), :], (((1,), (1,)), ((), ())), preferred_element_type=F32) * scale
        sp = jnp.maximum(z, 0.0) + jnp.log(1.0 + jnp.exp(-jnp.abs(z)))
        log_keep = -sp
        log_beta = z - sp
        if masked:
            log_keep = jnp.where(causal, log_keep, 0.0)
        hi = log_keep.astype(BF16)
        lo = (log_keep - hi.astype(F32)).astype(BF16)
        after = (jnp.dot(hi, later, preferred_element_type=F32)
                 + jnp.dot(lo, later, preferred_element_type=F32))
        return log_beta + after, after[:, 0:1] + log_keep[:, 0:1]

    def back(j, logw, run, masked):
        w = jnp.exp(logw if run is None else logw + run)
        if masked:
            w = jnp.where(causal, w, 0.0)
        return jnp.dot(w.astype(BF16), v_ref[rows_of(j), :], preferred_element_type=F32)

    @pl.when(i == 0)
    def _():
        logw, _ = front(0, True)
        o_ref[...] = back(0, logw, None, True).astype(o_ref.dtype)

    @pl.when(i > 0)
    def _():
        logw_d, tot_d = front(i, True)
        logw_p, tot_p = front(i - 1, False)
        acc_ref[...] = back(i, logw_d, None, True) + back(i - 1, logw_p, tot_d, False)
        run0 = tot_d + tot_p
        run_ref[...] = run0

        def cond(state):
            j, top = state
            return jnp.logical_and(j >= 0, top > -SB_SKIP)

        def body(state):
            j, _ = state
            logw, tot = front(j, False)
            run = run_ref[...]
            acc_ref[...] += back(j, logw, run, False)
            run = run + tot
            run_ref[...] = run
            return j - 1, jnp.max(run)

        lax.while_loop(cond, body, (i - 2, jnp.max(run0)))
        o_ref[...] = acc_ref[...].astype(o_ref.dtype)


def stick_breaking(qkv, *, n_heads, q_col, k_col, v_col, tq=256):
    t_len = qkv.shape[0]
    return pl.pallas_call(
        functools.partial(_sb_kernel, tq=tq),
        out_shape=jax.ShapeDtypeStruct((t_len, n_heads * HEAD_DIM), BF16),
        grid=(n_heads, t_len // tq),
        in_specs=[pl.BlockSpec((tq, HEAD_DIM), lambda h, i: (i, q_col + h)),
                  pl.BlockSpec((t_len, HEAD_DIM), lambda h, i: (0, k_col + h)),
                  pl.BlockSpec((t_len, HEAD_DIM), lambda h, i: (0, v_col + h))],
        out_specs=pl.BlockSpec((tq, HEAD_DIM), lambda h, i: (i, h)),
        scratch_shapes=[pltpu.VMEM((tq, HEAD_DIM), F32), pltpu.VMEM((tq, 1), F32)],
        compiler_params=_params(2),
        name="stick_breaking",
    )(qkv, qkv, qkv)


def _layer_norm(y, g, b):
    mu = jnp.mean(y, axis=-1, keepdims=True)
    yc = y - mu
    var = jnp.mean(yc * yc, axis=-1, keepdims=True)
    return yc * lax.rsqrt(var + LN_EPS) * g + b


def _mixout_kernel(main_ref, qm_ref, mkv_ref, wout_ref, h_ref, g_ref, b_ref, rwh_ref, rwl_ref, rb_ref,
                   o_ref, logit_ref, *, alpha):
    main_w = main_ref.shape[1]
    mix = jnp.dot(main_ref[...], wout_ref[0:main_w, :], preferred_element_type=F32)
    q = qm_ref[...].astype(BF16)
    heads = []
    for hd in range(MEM_HEADS):
        lo, hi = hd * HEAD_DIM, (hd + 1) * HEAD_DIM
        kh = mkv_ref[:, lo:hi]
        vh = mkv_ref[:, MEM_WIDTH + lo:MEM_WIDTH + hi]
        s = lax.dot_general(q[:, lo:hi], kh, (((1,), (1,)), ((), ())),
                            preferred_element_type=F32) * (HEAD_DIM ** -0.5)
        e = jnp.exp(s - jnp.max(s, axis=-1, keepdims=True))
        p = e * (1.0 / jnp.sum(e, axis=-1, keepdims=True))
        heads.append(jnp.dot(p.astype(BF16), vh, preferred_element_type=F32).astype(BF16))
    mem_out = jnp.concatenate(heads, axis=-1)
    mix = mix + jnp.dot(mem_out, wout_ref[main_w:, :], preferred_element_type=F32)
    h1 = _layer_norm(alpha * h_ref[...] + mix, g_ref[...], b_ref[...])
    o_ref[...] = h1
    hh = h1.astype(BF16)
    hl = (h1 - hh.astype(F32)).astype(BF16)
    logit_ref[...] = (jnp.dot(hh, rwh_ref[...], preferred_element_type=F32)
                      + jnp.dot(hl, rwh_ref[...], preferred_element_type=F32)
                      + jnp.dot(hh, rwl_ref[...], preferred_element_type=F32) + rb_ref[...])


def mixout(main, qsrc, q_col, mem_kv, w_out, h, ln_g, ln_b, router_w, router_b, *, alpha, tm=256):
    t_len, d = h.shape
    main_w = main.shape[1]
    n_e = router_w.shape[1]
    rw = jnp.zeros((d, LANES), F32).at[:, :n_e].set(router_w)
    rwh = rw.astype(BF16)
    rwl = (rw - rwh.astype(F32)).astype(BF16)
    rb = jnp.full((1, LANES), -jnp.inf, F32).at[0, :n_e].set(router_b)
    full = lambda a: pl.BlockSpec(a.shape, lambda i: (0, 0))
    args = (main, qsrc, mem_kv, w_out.astype(BF16), h, ln_g.reshape(1, d), ln_b.reshape(1, d), rwh, rwl, rb)
    return pl.pallas_call(
        functools.partial(_mixout_kernel, alpha=alpha),
        out_shape=(jax.ShapeDtypeStruct((t_len, d), F32), jax.ShapeDtypeStruct((t_len, LANES), F32)),
        grid=(t_len // tm,),
        in_specs=[pl.BlockSpec((tm, main_w), lambda i: (i, 0)),
                  pl.BlockSpec((tm, MEM_WIDTH), lambda i: (i, q_col)),
                  full(mem_kv), full(args[3]),
                  pl.BlockSpec((tm, d), lambda i: (i, 0)),
                  full(args[5]), full(args[6]), full(rwh), full(rwl), full(rb)],
        out_specs=(pl.BlockSpec((tm, d), lambda i: (i, 0)), pl.BlockSpec((tm, LANES), lambda i: (i, 0))),
        compiler_params=_params(1),
        name="mixout",
    )(*args)


def _router_kernel(logit_ref, route_ref, cnt_ref, carry_ref, *, tb):
    @pl.when(pl.program_id(0) == 0)
    def _():
        carry_ref[...] = jnp.zeros_like(carry_ref)

    vals = logit_ref[...]
    lane = lax.broadcasted_iota(I32, (tb, LANES), 1).astype(F32)
    sels, tops, ids = [], [], []
    for _ in range(TOP_K):
        m = jnp.max(vals, axis=-1, keepdims=True)
        idx = jnp.min(jnp.where(vals == m, lane, float(LANES)), axis=-1, keepdims=True)
        sel = lane == idx
        vals = jnp.where(sel, -jnp.inf, vals)
        sels.append(sel)
        tops.append(m)
        ids.append(idx)
    exps = [jnp.exp(m - tops[0]) for m in tops]
    inv = 1.0 / (exps[0] + exps[1] + exps[2] + exps[3])
    onehot = jnp.zeros((tb, LANES), F32)
    for sel in sels:
        onehot = onehot + sel.astype(F32)
    rows = lax.broadcasted_iota(I32, (tb, tb), 0)
    cols = lax.broadcasted_iota(I32, (tb, tb), 1)
    before = (cols < rows).astype(BF16)
    cum = jnp.dot(before, onehot.astype(BF16), preferred_element_type=F32) + carry_ref[...]
    out = jnp.zeros((tb, LANES), F32)
    for k in range(TOP_K):
        pos = jnp.sum(jnp.where(sels[k], cum, 0.0), axis=-1, keepdims=True)
        out = jnp.where(lane == k, ids[k], out)
        out = jnp.where(lane == TOP_K + k, exps[k] * inv, out)
        out = jnp.where(lane == 2 * TOP_K + k, pos, out)
    route_ref[...] = out
    carry_ref[...] += jnp.sum(onehot, axis=0, keepdims=True)
    cnt_ref[...] = jnp.broadcast_to(carry_ref[...], cnt_ref.shape)


def router(logits, *, tb=512):
    t_len = logits.shape[0]
    return pl.pallas_call(
        functools.partial(_router_kernel, tb=tb),
        out_shape=(jax.ShapeDtypeStruct((t_len, LANES), F32), jax.ShapeDtypeStruct((SUBLANES, LANES), F32)),
        grid=(t_len // tb,),
        in_specs=[pl.BlockSpec((tb, LANES), lambda i: (i, 0))],
        out_specs=(pl.BlockSpec((tb, LANES), lambda i: (i, 0)), pl.BlockSpec((SUBLANES, LANES), lambda i: (0, 0))),
        scratch_shapes=[pltpu.VMEM((1, LANES), F32)],
        compiler_params=_params(1),
        name="router",
    )(logits)


def _dispatch_kernel(te_ref, na_ref, dest_ref, h_ref, xs_ref, zero_ref, sem, zsem, *, tb, tm, n_tiles):
    @pl.when(pl.program_id(0) == 0)
    def _():
        zero_ref[...] = jnp.zeros_like(zero_ref)

        def has_padding(j):
            return jnp.logical_or(j >= na_ref[0] - 1, te_ref[j] != te_ref[jnp.minimum(j + 1, n_tiles - 1)])

        def zero_copy(j):
            return pltpu.make_async_copy(zero_ref, xs_ref.at[pl.ds(pl.multiple_of(j * tm, tm), tm), :], zsem)

        def zstart(j, c):
            @pl.when(has_padding(j))
            def _():
                zero_copy(j).start()
            return c

        def zwait(j, c):
            @pl.when(has_padding(j))
            def _():
                zero_copy(j).wait()
            return c

        lax.fori_loop(0, n_tiles, zstart, 0)
        lax.fori_loop(0, n_tiles, zwait, 0)

    def start(r, c):
        for k in range(TOP_K):
            slot = dest_ref[0, 0, r * TOP_K + k]
            pltpu.make_async_copy(h_ref.at[pl.ds(r, 1), :], xs_ref.at[pl.ds(slot, 1), :], sem).start()
        return c

    lax.fori_loop(0, tb, start, 0)
    rows = xs_ref.at[pl.ds(0, tb * TOP_K), :]
    pltpu.make_async_copy(rows, rows, sem).wait()


def dispatch(h, dest, tile_e, n_active, n_rows, *, tm, tb=256):
    t_len, d = h.shape
    dest3 = dest.reshape(t_len // tb, 1, tb * TOP_K)
    return pl.pallas_call(
        functools.partial(_dispatch_kernel, tb=tb, tm=tm, n_tiles=n_rows // tm),
        out_shape=jax.ShapeDtypeStruct((n_rows, d), F32),
        grid_spec=pltpu.PrefetchScalarGridSpec(
            num_scalar_prefetch=2,
            grid=(t_len // tb,),
            in_specs=[pl.BlockSpec((1, 1, tb * TOP_K), lambda i, te, na: (i, 0, 0), memory_space=pltpu.SMEM),
                      pl.BlockSpec((tb, d), lambda i, te, na: (i, 0))],
            out_specs=pl.BlockSpec(memory_space=pl.ANY),
            scratch_shapes=[pltpu.VMEM((tm, d), F32), pltpu.SemaphoreType.DMA(()), pltpu.SemaphoreType.DMA(())]),
        compiler_params=_params(1),
        name="dispatch",
    )(tile_e, n_active, dest3, h)


def _tile_idx(i, te_ref, na_ref):
    return jnp.minimum(i, na_ref[0] - 1)


def _zero_inactive(i, na_ref, o_ref):
    @pl.when(i >= na_ref[0])
    def _():
        o_ref[...] = jnp.zeros_like(o_ref)


def _expert_changed(i, te_ref):
    return jnp.logical_or(i == 0, te_ref[i] != te_ref[jnp.maximum(i - 1, 0)])


def _gateup_kernel(te_ref, na_ref, x_ref, w_ref, b_ref, o_ref, wb_ref):
    i = pl.program_id(1)
    nh = w_ref.shape[1]
    _zero_inactive(i, na_ref, o_ref)

    @pl.when(i < na_ref[0])
    def _():
        @pl.when(_expert_changed(i, te_ref))
        def _():
            rows = lax.broadcasted_iota(I32, (GU, GU), 0)
            cols = lax.broadcasted_iota(I32, (GU, GU), 1)
            src = jnp.where(cols < LANES, 2 * cols, 2 * (cols - LANES) + 1)
            perm = (rows == src).astype(BF16)
            for n in range(0, nh, GU):
                wb_ref[:, n:n + GU] = jnp.dot(w_ref[:, n:n + GU].astype(BF16), perm,
                                              preferred_element_type=F32).astype(BF16)

        x = x_ref[...].astype(BF16)
        for n in range(0, nh, GU):
            gu = jnp.dot(x, wb_ref[:, n:n + GU], preferred_element_type=F32) + b_ref[:, n:n + GU]
            gate = jnp.minimum(gu[:, :LANES], SWIGLU_LIMIT)
            up = jnp.clip(gu[:, LANES:], -SWIGLU_LIMIT, SWIGLU_LIMIT)
            act = (up + 1.0) * gate * jax.nn.sigmoid(SWIGLU_ALPHA * gate)
            o_ref[:, n // 2:n // 2 + LANES] = act.astype(o_ref.dtype)


def moe_gateup(tile_e, n_active, x, w, b, *, tm):
    n_rows, k = x.shape
    n2 = w.shape[2]
    nh = n2 // 2
    return pl.pallas_call(
        _gateup_kernel,
        out_shape=jax.ShapeDtypeStruct((n_rows, n2 // 2), BF16),
        grid_spec=pltpu.PrefetchScalarGridSpec(
            num_scalar_prefetch=2,
            grid=(2, n_rows // tm),
            in_specs=[pl.BlockSpec((tm, k), lambda hf, i, te, na: (_tile_idx(i, te, na), 0)),
                      pl.BlockSpec((None, k, nh), lambda hf, i, te, na: (te[_tile_idx(i, te, na)], 0, hf)),
                      pl.BlockSpec((None, 1, nh), lambda hf, i, te, na: (te[_tile_idx(i, te, na)], 0, hf))],
            out_specs=pl.BlockSpec((tm, nh // 2), lambda hf, i, te, na: (i, hf)),
            scratch_shapes=[pltpu.VMEM((k, nh), BF16)]),
        compiler_params=_params(2),
        name="moe_gateup",
    )(tile_e, n_active, x, w, b)


def _down_kernel(te_ref, na_ref, x_ref, w_ref, b_ref, o_ref, wb_ref, *, tn):
    i = pl.program_id(0)
    _zero_inactive(i, na_ref, o_ref)

    @pl.when(i < na_ref[0])
    def _():
        @pl.when(_expert_changed(i, te_ref))
        def _():
            wb_ref[...] = w_ref[...].astype(BF16)

        x = x_ref[...]
        for n in range(0, w_ref.shape[1], tn):
            o_ref[:, n:n + tn] = jnp.dot(x, wb_ref[:, n:n + tn], preferred_element_type=F32) + b_ref[:, n:n + tn]


def moe_down(tile_e, n_active, x, w, b, *, tm, tn=256):
    n_rows, k = x.shape
    n_out = w.shape[2]
    return pl.pallas_call(
        functools.partial(_down_kernel, tn=tn),
        out_shape=jax.ShapeDtypeStruct((n_rows, n_out), F32),
        grid_spec=pltpu.PrefetchScalarGridSpec(
            num_scalar_prefetch=2,
            grid=(n_rows // tm,),
            in_specs=[pl.BlockSpec((tm, k), lambda i, te, na: (_tile_idx(i, te, na), 0)),
                      pl.BlockSpec((None, k, n_out), lambda i, te, na: (te[_tile_idx(i, te, na)], 0, 0)),
                      pl.BlockSpec((None, 1, n_out), lambda i, te, na: (te[_tile_idx(i, te, na)], 0, 0))],
            out_specs=pl.BlockSpec((tm, n_out), lambda i, te, na: (i, 0)),
            scratch_shapes=[pltpu.VMEM((k, n_out), BF16)]),
        compiler_params=_params(1),
        name="moe_down",
    )(tile_e, n_active, x, w, b)


def _combine_kernel(dest_ref, next_ref, gate_ref, h_ref, g_ref, b_ref, y_ref, o_ref, buf_ref, sems, *, tb, alpha):
    i = pl.program_id(0)
    slot = i % 2

    def gather(idx_ref, s):
        def start(r, c):
            for k in range(TOP_K):
                row = idx_ref[0, 0, r * TOP_K + k]
                pltpu.make_async_copy(y_ref.at[pl.ds(row, 1), :], buf_ref.at[s, k, pl.ds(r, 1), :],
                                      sems.at[s]).start()
            return c
        lax.fori_loop(0, tb, start, 0)

    @pl.when(i == 0)
    def _():
        gather(dest_ref, 0)

    @pl.when(i + 1 < pl.num_programs(0))
    def _():
        gather(next_ref, 1 - slot)

    pltpu.make_async_copy(buf_ref.at[slot], buf_ref.at[slot], sems.at[slot]).wait()
    gates = gate_ref[...]
    ffn = buf_ref[slot, 0] * gates[:, TOP_K:TOP_K + 1]
    for k in range(1, TOP_K):
        ffn = ffn + buf_ref[slot, k] * gates[:, TOP_K + k:TOP_K + k + 1]
    o_ref[...] = _layer_norm(alpha * h_ref[...] + ffn, g_ref[...], b_ref[...])


def combine(y, dest, route, h, ln_g, ln_b, *, alpha, tb=256):
    t_len, d = h.shape
    nb = t_len // tb
    dest3 = dest.reshape(nb, 1, tb * TOP_K)
    idx_spec = lambda f: pl.BlockSpec((1, 1, tb * TOP_K), f, memory_space=pltpu.SMEM)
    return pl.pallas_call(
        functools.partial(_combine_kernel, tb=tb, alpha=alpha),
        out_shape=jax.ShapeDtypeStruct((t_len, d), F32),
        grid=(nb,),
        in_specs=[idx_spec(lambda i: (i, 0, 0)),
                  idx_spec(lambda i: (jnp.minimum(i + 1, nb - 1), 0, 0)),
                  pl.BlockSpec((tb, LANES), lambda i: (i, 0)),
                  pl.BlockSpec((tb, d), lambda i: (i, 0)),
                  pl.BlockSpec((1, d), lambda i: (0, 0)),
                  pl.BlockSpec((1, d), lambda i: (0, 0)),
                  pl.BlockSpec(memory_space=pl.ANY)],
        out_specs=pl.BlockSpec((tb, d), lambda i: (i, 0)),
        scratch_shapes=[pltpu.VMEM((2, TOP_K, tb, d), F32), pltpu.SemaphoreType.DMA((2,))],
        compiler_params=_params(1),
        name="combine",
    )(dest3, dest3, route, h, ln_g.reshape(1, d), ln_b.reshape(1, d), y)


def moe(h1, logits, layer, w_gate_up, b_gate_up, w_down, b_down, ln_g, ln_b, *, alpha, tm=512):
    t_len, d = h1.shape
    n_e = w_gate_up.shape[1]
    route, cnt = router(logits)
    eid = route[:, 0:TOP_K].astype(I32)
    pos = route[:, 2 * TOP_K:3 * TOP_K].astype(I32)
    counts = cnt[0, :n_e].astype(I32)
    padded = (counts + tm - 1) // tm * tm
    pend = jnp.cumsum(padded)
    pstart = pend - padded
    onehot = eid[:, :, None] == jnp.arange(n_e, dtype=I32)[None, None, :]
    dest = jnp.sum(jnp.where(onehot, pstart[None, None, :], 0), axis=-1) + pos
    n_tiles = t_len * TOP_K // tm + n_e
    tile_start = jnp.arange(n_tiles, dtype=I32) * tm
    tile_e = jnp.minimum(jnp.sum((pend[None, :] <= tile_start[:, None]).astype(I32), axis=1), n_e - 1)
    n_active = (pend[-1:] // tm).astype(I32)

    xs = dispatch(h1, dest, tile_e, n_active, n_tiles * tm, tm=tm)
    bgu = b_gate_up.reshape(-1, n_e, b_gate_up.shape[-1] // GU, LANES, 2).swapaxes(-1, -2)
    bgu = bgu.reshape(-1, 1, b_gate_up.shape[-1])
    tile_w = tile_e + layer * n_e
    act = moe_gateup(tile_w, n_active, xs, w_gate_up.reshape((-1,) + w_gate_up.shape[2:]), bgu, tm=tm)
    y = moe_down(tile_w, n_active, act, w_down.reshape((-1,) + w_down.shape[2:]),
                 b_down.reshape(-1, 1, b_down.shape[-1]), tm=tm)
    return combine(y, dest, route, h1, ln_g, ln_b, alpha=alpha)


def kernel(x, mem, a_w_in, a_conv_w, a_conv_b, a_rg_w, a_rg_b, a_ig_w, a_ig_b, a_lambda, a_w_out, b_w_q, b_w_out, w_kv_shared, mem_w_kv, ln1_g, ln1_b, ln2_g, ln2_b, router_w, router_b, w_gate_up, b_gate_up, w_down, b_down):
    bsz, seq, d = x.shape
    depth = ln1_g.shape[0]
    n_a = a_w_in.shape[0]
    alpha = float((2 * depth) ** 0.25)
    outs = []
    for bi in range(bsz):
        h = x[bi]
        mem_b = mem[bi]
        qkv = None
        for layer in range(depth):
            mem_kv = dense(mem_b, mem_w_kv[layer].astype(BF16), BF16, tm=mem_b.shape[0], tnb=2 * MEM_WIDTH)
            if layer < n_a:
                lw = a_rg_w.shape[1] * LANES
                proj = dense(h, a_w_in[layer].astype(BF16), F32, tm=512, tnb=(2 * lw + MEM_WIDTH) // 2)
                main = rglru(proj, a_conv_w[layer], a_conv_b[layer], a_rg_w[layer], a_rg_b[layer],
                             a_ig_w[layer], a_ig_b[layer], a_lambda[layer])
                qsrc, q_col, w_out = proj, 2 * lw // MEM_WIDTH, a_w_out[layer]
            else:
                j = layer - n_a
                sbw = w_kv_shared.shape[1] // 2
                n_heads = sbw // HEAD_DIM
                if j == 0:
                    w_cat = jnp.concatenate([w_kv_shared, b_w_q[j]], axis=1).astype(BF16)
                    qkv = dense(h, w_cat, BF16, tm=512, tnb=w_cat.shape[1] // 4)
                else:
                    q_only = dense(h, b_w_q[j].astype(BF16), BF16, tm=512, tnb=b_w_q[j].shape[1] // 2)
                    qkv = jnp.concatenate([qkv[:, :2 * sbw], q_only], axis=1)
                main = stick_breaking(qkv, n_heads=n_heads, q_col=2 * n_heads, k_col=0, v_col=n_heads)
                qsrc, q_col, w_out = qkv, (3 * sbw) // MEM_WIDTH, b_w_out[j]
            h1, logits = mixout(main, qsrc, q_col, mem_kv, w_out, h, ln1_g[layer], ln1_b[layer],
                                router_w[layer], router_b[layer], alpha=alpha)
            h = moe(h1, logits, layer, w_gate_up, b_gate_up, w_down, b_down,
                    ln2_g[layer], ln2_b[layer], alpha=alpha)
        outs.append(h)
    return jnp.stack(outs, axis=0)
```

```python
import functools

import jax
import jax.numpy as jnp
from jax import lax
from jax.experimental import pallas as pl
from jax.experimental.pallas import tpu as pltpu

F32 = jnp.float32
BF16 = jnp.bfloat16
I32 = jnp.int32

HEAD_DIM = 128
MEM_HEADS = 4
MEM_WIDTH = MEM_HEADS * HEAD_DIM
CONV_WIDTH = 4
LRU_C = 8.0
N_EXPERTS = 32
TOP_K = 4
SWIGLU_LIMIT = 7.0
SWIGLU_ALPHA = 1.702
LN_EPS = 1e-5
LANES = 128
SUBLANES = 8
GU = 2 * LANES
VMEM_LIMIT = 56 * 1024 * 1024
SB_SKIP = 100.0
SB_HEADS_PER_STEP = 2


def _params(n_axes, vmem=VMEM_LIMIT):
    return pltpu.CompilerParams(dimension_semantics=("arbitrary",) * n_axes, vmem_limit_bytes=vmem)


def _dense_kernel(x_ref, w_ref, o_ref, *, tn):
    xb = x_ref[...].astype(BF16)
    for n in range(0, w_ref.shape[1], tn):
        o_ref[:, n:n + tn] = jnp.dot(xb, w_ref[:, n:n + tn], preferred_element_type=F32).astype(o_ref.dtype)


def dense(x, w, out_dtype, *, tm, tn=256):
    m, k = x.shape
    n = w.shape[1]
    return pl.pallas_call(
        functools.partial(_dense_kernel, tn=tn),
        out_shape=jax.ShapeDtypeStruct((m, n), out_dtype),
        grid=(m // tm,),
        in_specs=[pl.BlockSpec((tm, k), lambda i: (i, 0)),
                  pl.BlockSpec((k, n), lambda i: (0, 0), pipeline_mode=pl.Buffered(1))],
        out_specs=pl.BlockSpec((tm, n), lambda i: (i, 0)),
        compiler_params=_params(1),
        name="dense",
    )(x, w)


def _softplus(x):
    return jnp.maximum(x, 0.0) + jnp.log1p(jnp.exp(-jnp.abs(x)))


def _gelu_tanh(x):
    return 0.5 * x * (1.0 + jnp.tanh(0.7978845608028654 * (x + 0.044715 * (x * x * x))))


def _rglru_kernel(gate_ref, u_ref, cw_ref, cb_ref, rgw_ref, rgb_ref, igw_ref, igb_ref, lam_ref,
                  o_ref, hist_ref, carry_ref, *, tt):
    t = pl.program_id(1)

    @pl.when(t == 0)
    def _():
        hist_ref[...] = jnp.zeros_like(hist_ref)
        carry_ref[...] = jnp.zeros_like(carry_ref)

    u = u_ref[...]
    ext = jnp.concatenate([hist_ref[...], u], axis=0)
    cw = cw_ref[...]
    xc = cb_ref[...] + u * cw[CONV_WIDTH - 1:CONV_WIDTH, :]
    for d in range(1, CONV_WIDTH):
        shifted = pltpu.roll(ext, d, axis=0)[SUBLANES:, :]
        xc = xc + shifted * cw[CONV_WIDTH - 1 - d:CONV_WIDTH - d, :]
    hist_ref[...] = u[tt - SUBLANES:, :]

    xcb = xc.astype(BF16)
    r = jax.nn.sigmoid(jnp.dot(xcb, rgw_ref[...], preferred_element_type=F32) + rgb_ref[...])
    ig = jax.nn.sigmoid(jnp.dot(xcb, igw_ref[...], preferred_element_type=F32) + igb_ref[...])
    log_a = (-LRU_C) * r * _softplus(-lam_ref[...])
    a = jnp.exp(log_a)
    b = jnp.sqrt(-jnp.tanh(log_a) * (a * a + 1.0)) * (ig * xc)

    groups = tt // SUBLANES
    a = a.reshape(groups, SUBLANES, LANES)
    b = b.reshape(groups, SUBLANES, LANES)
    sub = lax.broadcasted_iota(I32, (groups, SUBLANES, LANES), 1)
    d = 1
    while d < SUBLANES:
        keep = sub >= d
        a_sh = jnp.where(keep, pltpu.roll(a, d, axis=1), 1.0)
        b_sh = jnp.where(keep, pltpu.roll(b, d, axis=1), 0.0)
        b = a * b_sh + b
        a = a * a_sh
        d *= 2
    a = a.reshape(tt, LANES)
    b = b.reshape(tt, LANES)
    carry = carry_ref[...]
    hs = []
    for g in range(tt // SUBLANES):
        rows = slice(g * SUBLANES, (g + 1) * SUBLANES)
        hs.append(a[rows, :] * carry + b[rows, :])
        carry = hs[-1][SUBLANES - 1:SUBLANES, :]
    carry_ref[...] = carry
    o_ref[...] = (jnp.concatenate(hs, axis=0) * _gelu_tanh(gate_ref[...])).astype(o_ref.dtype)


def rglru(proj, conv_w, conv_b, rg_w, rg_b, ig_w, ig_b, lam, *, tt=512):
    t_len = proj.shape[0]
    nblk = rg_w.shape[0]
    width = nblk * LANES
    vec = lambda v: v.reshape(1, width)
    chan = lambda rows: pl.BlockSpec((rows, LANES), lambda n, t: (0, n))
    return pl.pallas_call(
        functools.partial(_rglru_kernel, tt=tt),
        out_shape=jax.ShapeDtypeStruct((t_len, width), BF16),
        grid=(nblk, t_len // tt),
        in_specs=[pl.BlockSpec((tt, LANES), lambda n, t: (t, n)),
                  pl.BlockSpec((tt, LANES), lambda n, t: (t, nblk + n)),
                  chan(CONV_WIDTH), chan(1),
                  pl.BlockSpec((None, LANES, LANES), lambda n, t: (n, 0, 0)), chan(1),
                  pl.BlockSpec((None, LANES, LANES), lambda n, t: (n, 0, 0)), chan(1),
                  chan(1)],
        out_specs=pl.BlockSpec((tt, LANES), lambda n, t: (t, n)),
        scratch_shapes=[pltpu.VMEM((SUBLANES, LANES), F32), pltpu.VMEM((1, LANES), F32)],
        compiler_params=_params(2),
        name="rglru",
    )(proj, proj, conv_w, vec(conv_b), rg_w.astype(BF16), vec(rg_b), ig_w.astype(BF16), vec(ig_b), vec(lam))


def _sb_kernel(q_ref, k_ref, v_ref, o_ref, acc_ref, run_ref, *, tq):
    i = pl.program_id(1)
    scale = HEAD_DIM ** -0.5
    rows = lax.broadcasted_iota(I32, (tq, tq), 0)
    cols = lax.broadcasted_iota(I32, (tq, tq), 1)
    later = (rows > cols).astype(BF16)
    causal = cols < rows
    heads = range(SB_HEADS_PER_STEP)

    def rows_of(j):
        return pl.ds(j * tq if isinstance(j, int) else pl.multiple_of(j * tq, tq), tq)

    def lanes_of(hd):
        return slice(hd * HEAD_DIM, (hd + 1) * HEAD_DIM)

    def front(hd, j, masked):
        z = lax.dot_general(q_ref[:, lanes_of(hd)], k_ref[rows_of(j), lanes_of(hd)], (((1,), (1,)), ((), ())),
                            preferred_element_type=F32) * scale
        sp = jnp.maximum(z, 0.0) + jnp.log(1.0 + jnp.exp(-jnp.abs(z)))
        log_keep = -sp
        log_beta = z - sp
        if masked:
            log_keep = jnp.where(causal, log_keep, 0.0)
        hi = log_keep.astype(BF16)
        lo = (log_keep - hi.astype(F32)).astype(BF16)
        after = (jnp.dot(hi, later, preferred_element_type=F32)
                 + jnp.dot(lo, later, preferred_element_type=F32))
        return log_beta + after, after[:, 0:1] + log_keep[:, 0:1]

    def back(hd, j, logw, run, masked):
        w = jnp.exp(logw if run is None else logw + run)
        if masked:
            w = jnp.where(causal, w, 0.0)
        return jnp.dot(w.astype(BF16), v_ref[rows_of(j), lanes_of(hd)], preferred_element_type=F32)

    @pl.when(i == 0)
    def _():
        for hd in heads:
            logw, _ = front(hd, 0, True)
            o_ref[:, lanes_of(hd)] = back(hd, 0, logw, None, True).astype(o_ref.dtype)

    @pl.when(i > 0)
    def _():
        tops = []
        for hd in heads:
            logw_d, tot_d = front(hd, i, True)
            logw_p, tot_p = front(hd, i - 1, False)
            acc_ref[:, lanes_of(hd)] = back(hd, i, logw_d, None, True) + back(hd, i - 1, logw_p, tot_d, False)
            run0 = tot_d + tot_p
            run_ref[hd] = run0
            tops.append(jnp.max(run0))

        def cond(state):
            j, top = state
            return jnp.logical_and(j >= 0, top > -SB_SKIP)

        def body(state):
            j, _ = state
            tops = []
            for hd in heads:
                logw, tot = front(hd, j, False)
                run = run_ref[hd]
                acc_ref[:, lanes_of(hd)] += back(hd, j, logw, run, False)
                run = run + tot
                run_ref[hd] = run
                tops.append(jnp.max(run))
            return j - 1, functools.reduce(jnp.maximum, tops)

        lax.while_loop(cond, body, (i - 2, functools.reduce(jnp.maximum, tops)))
        o_ref[...] = acc_ref[...].astype(o_ref.dtype)


def stick_breaking(qkv, *, n_heads, q_col, k_col, v_col, tq=256):
    t_len = qkv.shape[0]
    hg = SB_HEADS_PER_STEP
    width = hg * HEAD_DIM
    assert n_heads % hg == 0 and q_col % hg == 0 and k_col % hg == 0 and v_col % hg == 0
    return pl.pallas_call(
        functools.partial(_sb_kernel, tq=tq),
        out_shape=jax.ShapeDtypeStruct((t_len, n_heads * HEAD_DIM), BF16),
        grid=(n_heads // hg, t_len // tq),
        in_specs=[pl.BlockSpec((tq, width), lambda h, i: (i, q_col // hg + h)),
                  pl.BlockSpec((t_len, width), lambda h, i: (0, k_col // hg + h)),
                  pl.BlockSpec((t_len, width), lambda h, i: (0, v_col // hg + h))],
        out_specs=pl.BlockSpec((tq, width), lambda h, i: (i, h)),
        scratch_shapes=[pltpu.VMEM((tq, width), F32), pltpu.VMEM((hg, tq, 1), F32)],
        compiler_params=_params(2),
        name="stick_breaking",
    )(qkv, qkv, qkv)


def _layer_norm(y, g, b):
    mu = jnp.mean(y, axis=-1, keepdims=True)
    yc = y - mu
    var = jnp.mean(yc * yc, axis=-1, keepdims=True)
    return yc * lax.rsqrt(var + LN_EPS) * g + b


def _mixout_kernel(main_ref, qm_ref, mkv_ref, wout_ref, h_ref, g_ref, b_ref, rw2_ref, rb_ref,
                   o_ref, logit_ref, *, alpha):
    main_w = main_ref.shape[1]
    mix = jnp.dot(main_ref[...], wout_ref[0:main_w, :], preferred_element_type=F32)
    q = qm_ref[...].astype(BF16)
    heads = []
    for hd in range(MEM_HEADS):
        lo, hi = hd * HEAD_DIM, (hd + 1) * HEAD_DIM
        kh = mkv_ref[:, lo:hi]
        vh = mkv_ref[:, MEM_WIDTH + lo:MEM_WIDTH + hi]
        s = lax.dot_general(q[:, lo:hi], kh, (((1,), (1,)), ((), ())),
                            preferred_element_type=F32) * (HEAD_DIM ** -0.5)
        e = jnp.exp(s - jnp.max(s, axis=-1, keepdims=True))
        p = e * (1.0 / jnp.sum(e, axis=-1, keepdims=True))
        heads.append(jnp.dot(p.astype(BF16), vh, preferred_element_type=F32).astype(BF16))
    mem_out = jnp.concatenate(heads, axis=-1)
    mix = mix + jnp.dot(mem_out, wout_ref[main_w:, :], preferred_element_type=F32)
    h1 = _layer_norm(alpha * h_ref[...] + mix, g_ref[...], b_ref[...])
    o_ref[...] = h1
    hh = h1.astype(BF16)
    hl = (h1 - hh.astype(F32)).astype(BF16)
    both = jnp.dot(hh, rw2_ref[...], preferred_element_type=F32)
    logit_ref[...] = (both[:, :LANES] + both[:, LANES:]
                      + jnp.dot(hl, rw2_ref[:, :LANES], preferred_element_type=F32) + rb_ref[...])


def mixout(main, qsrc, q_col, mem_kv, w_out, h, ln_g, ln_b, router_w, router_b, *, alpha, tm=512):
    t_len, d = h.shape
    main_w = main.shape[1]
    n_e = router_w.shape[1]
    rw = jnp.zeros((d, LANES), F32).at[:, :n_e].set(router_w)
    rwh = rw.astype(BF16)
    rw2 = jnp.concatenate([rwh, (rw - rwh.astype(F32)).astype(BF16)], axis=1)
    rb = jnp.full((1, LANES), -jnp.inf, F32).at[0, :n_e].set(router_b)
    full = lambda a: pl.BlockSpec(a.shape, lambda i: (0, 0))
    args = (main, qsrc, mem_kv, w_out.astype(BF16), h, ln_g.reshape(1, d), ln_b.reshape(1, d), rw2, rb)
    return pl.pallas_call(
        functools.partial(_mixout_kernel, alpha=alpha),
        out_shape=(jax.ShapeDtypeStruct((t_len, d), F32), jax.ShapeDtypeStruct((t_len, LANES), F32)),
        grid=(t_len // tm,),
        in_specs=[pl.BlockSpec((tm, main_w), lambda i: (i, 0)),
                  pl.BlockSpec((tm, MEM_WIDTH), lambda i: (i, q_col)),
                  full(mem_kv), full(args[3]),
                  pl.BlockSpec((tm, d), lambda i: (i, 0)),
                  full(args[5]), full(args[6]), full(rw2), full(rb)],
        out_specs=(pl.BlockSpec((tm, d), lambda i: (i, 0)), pl.BlockSpec((tm, LANES), lambda i: (i, 0))),
        compiler_params=_params(1),
        name="mixout",
    )(*args)


def _router_kernel(logit_ref, route_ref, cnt_ref, carry_ref, *, tb):
    @pl.when(pl.program_id(0) == 0)
    def _():
        carry_ref[...] = jnp.zeros_like(carry_ref)

    vals = logit_ref[...]
    lane = lax.broadcasted_iota(I32, (tb, LANES), 1).astype(F32)
    sels, tops, ids = [], [], []
    for _ in range(TOP_K):
        m = jnp.max(vals, axis=-1, keepdims=True)
        idx = jnp.min(jnp.where(vals == m, lane, float(LANES)), axis=-1, keepdims=True)
        sel = lane == idx
        vals = jnp.where(sel, -jnp.inf, vals)
        sels.append(sel)
        tops.append(m)
        ids.append(idx)
    exps = [jnp.exp(m - tops[0]) for m in tops]
    inv = 1.0 / (exps[0] + exps[1] + exps[2] + exps[3])
    onehot = jnp.zeros((tb, LANES), F32)
    for sel in sels:
        onehot = onehot + sel.astype(F32)
    rows = lax.broadcasted_iota(I32, (tb, tb), 0)
    cols = lax.broadcasted_iota(I32, (tb, tb), 1)
    before = (cols < rows).astype(BF16)
    cum = jnp.dot(before, onehot.astype(BF16), preferred_element_type=F32) + carry_ref[...]
    out = jnp.zeros((tb, LANES), F32)
    for k in range(TOP_K):
        pos = jnp.sum(jnp.where(sels[k], cum, 0.0), axis=-1, keepdims=True)
        out = jnp.where(lane == k, ids[k], out)
        out = jnp.where(lane == TOP_K + k, exps[k] * inv, out)
        out = jnp.where(lane == 2 * TOP_K + k, pos, out)
    route_ref[...] = out
    carry_ref[...] += jnp.sum(onehot, axis=0, keepdims=True)
    cnt_ref[...] = jnp.broadcast_to(carry_ref[...], cnt_ref.shape)


def router(logits, *, tb=512):
    t_len = logits.shape[0]
    return pl.pallas_call(
        functools.partial(_router_kernel, tb=tb),
        out_shape=(jax.ShapeDtypeStruct((t_len, LANES), F32), jax.ShapeDtypeStruct((SUBLANES, LANES), F32)),
        grid=(t_len // tb,),
        in_specs=[pl.BlockSpec((tb, LANES), lambda i: (i, 0))],
        out_specs=(pl.BlockSpec((tb, LANES), lambda i: (i, 0)), pl.BlockSpec((SUBLANES, LANES), lambda i: (0, 0))),
        scratch_shapes=[pltpu.VMEM((1, LANES), F32)],
        compiler_params=_params(1),
        name="router",
    )(logits)


def _dispatch_kernel(te_ref, na_ref, dest_ref, h_ref, xs_ref, zero_ref, sem, zsem, *, tb, tm, n_tiles):
    @pl.when(pl.program_id(0) == 0)
    def _():
        zero_ref[...] = jnp.zeros_like(zero_ref)

        def has_padding(j):
            return jnp.logical_or(j >= na_ref[0] - 1, te_ref[j] != te_ref[jnp.minimum(j + 1, n_tiles - 1)])

        def zero_copy(j):
            return pltpu.make_async_copy(zero_ref, xs_ref.at[pl.ds(pl.multiple_of(j * tm, tm), tm), :], zsem)

        def zstart(j, c):
            @pl.when(has_padding(j))
            def _():
                zero_copy(j).start()
            return c

        def zwait(j, c):
            @pl.when(has_padding(j))
            def _():
                zero_copy(j).wait()
            return c

        lax.fori_loop(0, n_tiles, zstart, 0)
        lax.fori_loop(0, n_tiles, zwait, 0)

    def start(r, c):
        for k in range(TOP_K):
            slot = dest_ref[0, 0, r * TOP_K + k]
            pltpu.make_async_copy(h_ref.at[pl.ds(r, 1), :], xs_ref.at[pl.ds(slot, 1), :], sem).start()
        return c

    lax.fori_loop(0, tb, start, 0)
    rows = xs_ref.at[pl.ds(0, tb * TOP_K), :]
    pltpu.make_async_copy(rows, rows, sem).wait()


def dispatch(h, dest, tile_e, n_active, n_rows, *, tm, tb=256):
    t_len, d = h.shape
    dest3 = dest.reshape(t_len // tb, 1, tb * TOP_K)
    return pl.pallas_call(
        functools.partial(_dispatch_kernel, tb=tb, tm=tm, n_tiles=n_rows // tm),
        out_shape=jax.ShapeDtypeStruct((n_rows, d), F32),
        grid_spec=pltpu.PrefetchScalarGridSpec(
            num_scalar_prefetch=2,
            grid=(t_len // tb,),
            in_specs=[pl.BlockSpec((1, 1, tb * TOP_K), lambda i, te, na: (i, 0, 0), memory_space=pltpu.SMEM),
                      pl.BlockSpec((tb, d), lambda i, te, na: (i, 0))],
            out_specs=pl.BlockSpec(memory_space=pl.ANY),
            scratch_shapes=[pltpu.VMEM((tm, d), F32), pltpu.SemaphoreType.DMA(()), pltpu.SemaphoreType.DMA(())]),
        compiler_params=_params(1),
        name="dispatch",
    )(tile_e, n_active, dest3, h)


def _tile_idx(i, te_ref, na_ref):
    return jnp.minimum(i, na_ref[0] - 1)


def _zero_inactive(i, na_ref, o_ref):
    @pl.when(i >= na_ref[0])
    def _():
        o_ref[...] = jnp.zeros_like(o_ref)


def _expert_changed(i, te_ref):
    return jnp.logical_or(i == 0, te_ref[i] != te_ref[jnp.maximum(i - 1, 0)])


def _gateup_kernel(te_ref, na_ref, x_ref, w_ref, b_ref, o_ref, wb_ref):
    i = pl.program_id(1)
    nh = w_ref.shape[1]
    _zero_inactive(i, na_ref, o_ref)

    @pl.when(i < na_ref[0])
    def _():
        @pl.when(_expert_changed(i, te_ref))
        def _():
            rows = lax.broadcasted_iota(I32, (GU, GU), 0)
            cols = lax.broadcasted_iota(I32, (GU, GU), 1)
            src = jnp.where(cols < LANES, 2 * cols, 2 * (cols - LANES) + 1)
            perm = (rows == src).astype(BF16)
            for n in range(0, nh, GU):
                wb_ref[:, n:n + GU] = jnp.dot(w_ref[:, n:n + GU].astype(BF16), perm,
                                              preferred_element_type=F32).astype(BF16)

        x = x_ref[...].astype(BF16)
        for n in range(0, nh, GU):
            gu = jnp.dot(x, wb_ref[:, n:n + GU], preferred_element_type=F32) + b_ref[:, n:n + GU]
            gate = jnp.minimum(gu[:, :LANES], SWIGLU_LIMIT)
            up = jnp.clip(gu[:, LANES:], -SWIGLU_LIMIT, SWIGLU_LIMIT)
            act = (up + 1.0) * gate * jax.nn.sigmoid(SWIGLU_ALPHA * gate)
            o_ref[:, n // 2:n // 2 + LANES] = act.astype(o_ref.dtype)


def moe_gateup(tile_e, n_active, x, w, b, *, tm):
    n_rows, k = x.shape
    n2 = w.shape[2]
    nh = n2 // 2
    return pl.pallas_call(
        _gateup_kernel,
        out_shape=jax.ShapeDtypeStruct((n_rows, n2 // 2), BF16),
        grid_spec=pltpu.PrefetchScalarGridSpec(
            num_scalar_prefetch=2,
            grid=(2, n_rows // tm),
            in_specs=[pl.BlockSpec((tm, k), lambda hf, i, te, na: (_tile_idx(i, te, na), 0)),
                      pl.BlockSpec((None, k, nh), lambda hf, i, te, na: (te[_tile_idx(i, te, na)], 0, hf)),
                      pl.BlockSpec((None, 1, nh), lambda hf, i, te, na: (te[_tile_idx(i, te, na)], 0, hf))],
            out_specs=pl.BlockSpec((tm, nh // 2), lambda hf, i, te, na: (i, hf)),
            scratch_shapes=[pltpu.VMEM((k, nh), BF16)]),
        compiler_params=_params(2),
        name="moe_gateup",
    )(tile_e, n_active, x, w, b)


def _down_kernel(te_ref, na_ref, x_ref, w_ref, b_ref, o_ref, wb_ref, *, tn):
    i = pl.program_id(0)
    _zero_inactive(i, na_ref, o_ref)

    @pl.when(i < na_ref[0])
    def _():
        @pl.when(_expert_changed(i, te_ref))
        def _():
            wb_ref[...] = w_ref[...].astype(BF16)

        x = x_ref[...]
        for n in range(0, w_ref.shape[1], tn):
            o_ref[:, n:n + tn] = jnp.dot(x, wb_ref[:, n:n + tn], preferred_element_type=F32) + b_ref[:, n:n + tn]


def moe_down(tile_e, n_active, x, w, b, *, tm, tn=256):
    n_rows, k = x.shape
    n_out = w.shape[2]
    return pl.pallas_call(
        functools.partial(_down_kernel, tn=tn),
        out_shape=jax.ShapeDtypeStruct((n_rows, n_out), F32),
        grid_spec=pltpu.PrefetchScalarGridSpec(
            num_scalar_prefetch=2,
            grid=(n_rows // tm,),
            in_specs=[pl.BlockSpec((tm, k), lambda i, te, na: (_tile_idx(i, te, na), 0)),
                      pl.BlockSpec((None, k, n_out), lambda i, te, na: (te[_tile_idx(i, te, na)], 0, 0)),
                      pl.BlockSpec((None, 1, n_out), lambda i, te, na: (te[_tile_idx(i, te, na)], 0, 0))],
            out_specs=pl.BlockSpec((tm, n_out), lambda i, te, na: (i, 0)),
            scratch_shapes=[pltpu.VMEM((k, n_out), BF16)]),
        compiler_params=_params(1),
        name="moe_down",
    )(tile_e, n_active, x, w, b)


def _combine_kernel(dest_ref, next_ref, gate_ref, h_ref, g_ref, b_ref, y_ref, o_ref, buf_ref, sems, *, tb, alpha):
    i = pl.program_id(0)
    slot = i % 2

    def gather(idx_ref, s):
        def start(r, c):
            for k in range(TOP_K):
                row = idx_ref[0, 0, r * TOP_K + k]
                pltpu.make_async_copy(y_ref.at[pl.ds(row, 1), :], buf_ref.at[s, k, pl.ds(r, 1), :],
                                      sems.at[s]).start()
            return c
        lax.fori_loop(0, tb, start, 0)

    @pl.when(i == 0)
    def _():
        gather(dest_ref, 0)

    @pl.when(i + 1 < pl.num_programs(0))
    def _():
        gather(next_ref, 1 - slot)

    pltpu.make_async_copy(buf_ref.at[slot], buf_ref.at[slot], sems.at[slot]).wait()
    gates = gate_ref[...]
    ffn = buf_ref[slot, 0] * gates[:, TOP_K:TOP_K + 1]
    for k in range(1, TOP_K):
        ffn = ffn + buf_ref[slot, k] * gates[:, TOP_K + k:TOP_K + k + 1]
    o_ref[...] = _layer_norm(alpha * h_ref[...] + ffn, g_ref[...], b_ref[...])


def combine(y, dest, route, h, ln_g, ln_b, *, alpha, tb=256):
    t_len, d = h.shape
    nb = t_len // tb
    dest3 = dest.reshape(nb, 1, tb * TOP_K)
    idx_spec = lambda f: pl.BlockSpec((1, 1, tb * TOP_K), f, memory_space=pltpu.SMEM)
    return pl.pallas_call(
        functools.partial(_combine_kernel, tb=tb, alpha=alpha),
        out_shape=jax.ShapeDtypeStruct((t_len, d), F32),
        grid=(nb,),
        in_specs=[idx_spec(lambda i: (i, 0, 0)),
                  idx_spec(lambda i: (jnp.minimum(i + 1, nb - 1), 0, 0)),
                  pl.BlockSpec((tb, LANES), lambda i: (i, 0)),
                  pl.BlockSpec((tb, d), lambda i: (i, 0)),
                  pl.BlockSpec((1, d), lambda i: (0, 0)),
                  pl.BlockSpec((1, d), lambda i: (0, 0)),
                  pl.BlockSpec(memory_space=pl.ANY)],
        out_specs=pl.BlockSpec((tb, d), lambda i: (i, 0)),
        scratch_shapes=[pltpu.VMEM((2, TOP_K, tb, d), F32), pltpu.SemaphoreType.DMA((2,))],
        compiler_params=_params(1),
        name="combine",
    )(dest3, dest3, route, h, ln_g.reshape(1, d), ln_b.reshape(1, d), y)


def moe(h1, logits, layer, w_gate_up, b_gate_up, w_down, b_down, ln_g, ln_b, *, alpha, tm=512):
    t_len, d = h1.shape
    n_e = w_gate_up.shape[1]
    route, cnt = router(logits)
    eid = route[:, 0:TOP_K].astype(I32)
    pos = route[:, 2 * TOP_K:3 * TOP_K].astype(I32)
    counts = cnt[0, :n_e].astype(I32)
    padded = (counts + tm - 1) // tm * tm
    pend = jnp.cumsum(padded)
    pstart = pend - padded
    onehot = eid[:, :, None] == jnp.arange(n_e, dtype=I32)[None, None, :]
    dest = jnp.sum(jnp.where(onehot, pstart[None, None, :], 0), axis=-1) + pos
    n_tiles = t_len * TOP_K // tm + n_e
    tile_start = jnp.arange(n_tiles, dtype=I32) * tm
    tile_e = jnp.minimum(jnp.sum((pend[None, :] <= tile_start[:, None]).astype(I32), axis=1), n_e - 1)
    n_active = (pend[-1:] // tm).astype(I32)

    xs = dispatch(h1, dest, tile_e, n_active, n_tiles * tm, tm=tm)
    bgu = b_gate_up.reshape(-1, n_e, b_gate_up.shape[-1] // GU, LANES, 2).swapaxes(-1, -2)
    bgu = bgu.reshape(-1, 1, b_gate_up.shape[-1])
    tile_w = tile_e + layer * n_e
    act = moe_gateup(tile_w, n_active, xs, w_gate_up.reshape((-1,) + w_gate_up.shape[2:]), bgu, tm=tm)
    y = moe_down(tile_w, n_active, act, w_down.reshape((-1,) + w_down.shape[2:]),
                 b_down.reshape(-1, 1, b_down.shape[-1]), tm=tm)
    return combine(y, dest, route, h1, ln_g, ln_b, alpha=alpha)


def kernel(x, mem, a_w_in, a_conv_w, a_conv_b, a_rg_w, a_rg_b, a_ig_w, a_ig_b, a_lambda, a_w_out, b_w_q, b_w_out, w_kv_shared, mem_w_kv, ln1_g, ln1_b, ln2_g, ln2_b, router_w, router_b, w_gate_up, b_gate_up, w_down, b_down):
    bsz, seq, d = x.shape
    depth = ln1_g.shape[0]
    n_a = a_w_in.shape[0]
    alpha = float((2 * depth) ** 0.25)
    outs = []
    for bi in range(bsz):
        h = x[bi]
        mem_b = mem[bi]
        qkv = None
        for layer in range(depth):
            mem_kv = dense(mem_b, mem_w_kv[layer].astype(BF16), BF16, tm=mem_b.shape[0])
            if layer < n_a:
                lw = a_rg_w.shape[1] * LANES
                proj = dense(h, a_w_in[layer].astype(BF16), F32, tm=512)
                main = rglru(proj, a_conv_w[layer], a_conv_b[layer], a_rg_w[layer], a_rg_b[layer],
                             a_ig_w[layer], a_ig_b[layer], a_lambda[layer])
                qsrc, q_col, w_out = proj, 2 * lw // MEM_WIDTH, a_w_out[layer]
            else:
                j = layer - n_a
                sbw = w_kv_shared.shape[1] // 2
                n_heads = sbw // HEAD_DIM
                if j == 0:
                    w_cat = jnp.concatenate([w_kv_shared, b_w_q[j]], axis=1).astype(BF16)
                    qkv = dense(h, w_cat, BF16, tm=512)
                else:
                    q_only = dense(h, b_w_q[j].astype(BF16), BF16, tm=512)
                    qkv = jnp.concatenate([qkv[:, :2 * sbw], q_only], axis=1)
                main = stick_breaking(qkv, n_heads=n_heads, q_col=2 * n_heads, k_col=0, v_col=n_heads)
                qsrc, q_col, w_out = qkv, (3 * sbw) // MEM_WIDTH, b_w_out[j]
            h1, logits = mixout(main, qsrc, q_col, mem_kv, w_out, h, ln1_g[layer], ln1_b[layer],
                                router_w[layer], router_b[layer], alpha=alpha)
            h = moe(h1, logits, layer, w_gate_up, b_gate_up, w_down, b_down,
                    ln2_g[layer], ln2_b[layer], alpha=alpha)
        outs.append(h)
    return jnp.stack(outs, axis=0)
```

```python
import functools

import jax
import jax.numpy as jnp
from jax import lax
from jax.experimental import pallas as pl
from jax.experimental.pallas import tpu as pltpu

F32 = jnp.float32
BF16 = jnp.bfloat16
I32 = jnp.int32

HEAD_DIM = 128
MEM_HEADS = 4
MEM_WIDTH = MEM_HEADS * HEAD_DIM
CONV_WIDTH = 4
LRU_C = 8.0
N_EXPERTS = 32
TOP_K = 4
SWIGLU_LIMIT = 7.0
SWIGLU_ALPHA = 1.702
LN_EPS = 1e-5
LANES = 128
SUBLANES = 8
GU = 2 * LANES
VMEM_LIMIT = 56 * 1024 * 1024
SB_SKIP = 100.0
SB_HEADS_PER_STEP = 2


def _params(n_axes, vmem=VMEM_LIMIT):
    return pltpu.CompilerParams(dimension_semantics=("arbitrary",) * n_axes, vmem_limit_bytes=vmem)


def _dense_kernel(x_ref, w_ref, o_ref, *, tn):
    xb = x_ref[...].astype(BF16)
    for n in range(0, w_ref.shape[1], tn):
        o_ref[:, n:n + tn] = jnp.dot(xb, w_ref[:, n:n + tn], preferred_element_type=F32).astype(o_ref.dtype)


def dense(x, w, out_dtype, *, tm, tn=256):
    m, k = x.shape
    n = w.shape[1]
    return pl.pallas_call(
        functools.partial(_dense_kernel, tn=tn),
        out_shape=jax.ShapeDtypeStruct((m, n), out_dtype),
        grid=(m // tm,),
        in_specs=[pl.BlockSpec((tm, k), lambda i: (i, 0)),
                  pl.BlockSpec((k, n), lambda i: (0, 0), pipeline_mode=pl.Buffered(1))],
        out_specs=pl.BlockSpec((tm, n), lambda i: (i, 0)),
        compiler_params=_params(1),
        name="dense",
    )(x, w)


def _softplus(x):
    return jnp.maximum(x, 0.0) + jnp.log1p(jnp.exp(-jnp.abs(x)))


def _gelu_tanh(x):
    return 0.5 * x * (1.0 + jnp.tanh(0.7978845608028654 * (x + 0.044715 * (x * x * x))))


def _rglru_kernel(gate_ref, u_ref, cw_ref, cb_ref, rgw_ref, rgb_ref, igw_ref, igb_ref, lam_ref,
                  o_ref, hist_ref, carry_ref, *, tt):
    t = pl.program_id(1)

    @pl.when(t == 0)
    def _():
        hist_ref[...] = jnp.zeros_like(hist_ref)
        carry_ref[...] = jnp.zeros_like(carry_ref)

    u = u_ref[...]
    ext = jnp.concatenate([hist_ref[...], u], axis=0)
    cw = cw_ref[...]
    xc = cb_ref[...] + u * cw[CONV_WIDTH - 1:CONV_WIDTH, :]
    for d in range(1, CONV_WIDTH):
        shifted = pltpu.roll(ext, d, axis=0)[SUBLANES:, :]
        xc = xc + shifted * cw[CONV_WIDTH - 1 - d:CONV_WIDTH - d, :]
    hist_ref[...] = u[tt - SUBLANES:, :]

    xcb = xc.astype(BF16)
    r = jax.nn.sigmoid(jnp.dot(xcb, rgw_ref[...], preferred_element_type=F32) + rgb_ref[...])
    ig = jax.nn.sigmoid(jnp.dot(xcb, igw_ref[...], preferred_element_type=F32) + igb_ref[...])
    log_a = (-LRU_C) * r * _softplus(-lam_ref[...])
    a = jnp.exp(log_a)
    b = jnp.sqrt(-jnp.tanh(log_a) * (a * a + 1.0)) * (ig * xc)

    groups = tt // SUBLANES
    a = a.reshape(groups, SUBLANES, LANES)
    b = b.reshape(groups, SUBLANES, LANES)
    sub = lax.broadcasted_iota(I32, (groups, SUBLANES, LANES), 1)
    d = 1
    while d < SUBLANES:
        keep = sub >= d
        a_sh = jnp.where(keep, pltpu.roll(a, d, axis=1), 1.0)
        b_sh = jnp.where(keep, pltpu.roll(b, d, axis=1), 0.0)
        b = a * b_sh + b
        a = a * a_sh
        d *= 2
    a = a.reshape(tt, LANES)
    b = b.reshape(tt, LANES)
    carry = carry_ref[...]
    hs = []
    for g in range(tt // SUBLANES):
        rows = slice(g * SUBLANES, (g + 1) * SUBLANES)
        hs.append(a[rows, :] * carry + b[rows, :])
        carry = hs[-1][SUBLANES - 1:SUBLANES, :]
    carry_ref[...] = carry
    o_ref[...] = (jnp.concatenate(hs, axis=0) * _gelu_tanh(gate_ref[...])).astype(o_ref.dtype)


def rglru(proj, conv_w, conv_b, rg_w, rg_b, ig_w, ig_b, lam, *, tt=512):
    t_len = proj.shape[0]
    nblk = rg_w.shape[0]
    width = nblk * LANES
    vec = lambda v: v.reshape(1, width)
    chan = lambda rows: pl.BlockSpec((rows, LANES), lambda n, t: (0, n))
    return pl.pallas_call(
        functools.partial(_rglru_kernel, tt=tt),
        out_shape=jax.ShapeDtypeStruct((t_len, width), BF16),
        grid=(nblk, t_len // tt),
        in_specs=[pl.BlockSpec((tt, LANES), lambda n, t: (t, n)),
                  pl.BlockSpec((tt, LANES), lambda n, t: (t, nblk + n)),
                  chan(CONV_WIDTH), chan(1),
                  pl.BlockSpec((None, LANES, LANES), lambda n, t: (n, 0, 0)), chan(1),
                  pl.BlockSpec((None, LANES, LANES), lambda n, t: (n, 0, 0)), chan(1),
                  chan(1)],
        out_specs=pl.BlockSpec((tt, LANES), lambda n, t: (t, n)),
        scratch_shapes=[pltpu.VMEM((SUBLANES, LANES), F32), pltpu.VMEM((1, LANES), F32)],
        compiler_params=_params(2),
        name="rglru",
    )(proj, proj, conv_w, vec(conv_b), rg_w.astype(BF16), vec(rg_b), ig_w.astype(BF16), vec(ig_b), vec(lam))


def _sb_kernel(q_ref, k_ref, v_ref, o_ref, acc_ref, run_ref, *, tq):
    i = pl.program_id(1)
    scale = HEAD_DIM ** -0.5
    rows = lax.broadcasted_iota(I32, (tq, tq), 0)
    cols = lax.broadcasted_iota(I32, (tq, tq), 1)
    later = (rows > cols).astype(BF16)
    causal = cols < rows
    heads = range(SB_HEADS_PER_STEP)

    def rows_of(j):
        return pl.ds(j * tq if isinstance(j, int) else pl.multiple_of(j * tq, tq), tq)

    def lanes_of(hd):
        return slice(hd * HEAD_DIM, (hd + 1) * HEAD_DIM)

    def front(hd, j, masked):
        z = lax.dot_general(q_ref[:, lanes_of(hd)], k_ref[rows_of(j), lanes_of(hd)], (((1,), (1,)), ((), ())),
                            preferred_element_type=F32) * scale
        sp = jnp.maximum(z, 0.0) + jnp.log(1.0 + jnp.exp(-jnp.abs(z)))
        log_keep = -sp
        log_beta = z - sp
        if masked:
            log_keep = jnp.where(causal, log_keep, 0.0)
        hi = log_keep.astype(BF16)
        lo = (log_keep - hi.astype(F32)).astype(BF16)
        after = (jnp.dot(hi, later, preferred_element_type=F32)
                 + jnp.dot(lo, later, preferred_element_type=F32))
        return log_beta + after, after[:, 0:1] + log_keep[:, 0:1]

    def back(hd, j, logw, run, masked):
        w = jnp.exp(logw if run is None else logw + run)
        if masked:
            w = jnp.where(causal, w, 0.0)
        return jnp.dot(w.astype(BF16), v_ref[rows_of(j), lanes_of(hd)], preferred_element_type=F32)

    @pl.when(i == 0)
    def _():
        for hd in heads:
            logw, _ = front(hd, 0, True)
            o_ref[:, lanes_of(hd)] = back(hd, 0, logw, None, True).astype(o_ref.dtype)

    @pl.when(i > 0)
    def _():
        tops = []
        for hd in heads:
            logw_d, tot_d = front(hd, i, True)
            logw_p, tot_p = front(hd, i - 1, False)
            acc_ref[:, lanes_of(hd)] = back(hd, i, logw_d, None, True) + back(hd, i - 1, logw_p, tot_d, False)
            run0 = tot_d + tot_p
            run_ref[hd] = run0
            tops.append(jnp.max(run0))

        def cond(state):
            j, top = state
            return jnp.logical_and(j >= 0, top > -SB_SKIP)

        def body(state):
            j, _ = state
            tops = []
            for hd in heads:
                logw, tot = front(hd, j, False)
                run = run_ref[hd]
                acc_ref[:, lanes_of(hd)] += back(hd, j, logw, run, False)
                run = run + tot
                run_ref[hd] = run
                tops.append(jnp.max(run))
            return j - 1, functools.reduce(jnp.maximum, tops)

        lax.while_loop(cond, body, (i - 2, functools.reduce(jnp.maximum, tops)))
        o_ref[...] = acc_ref[...].astype(o_ref.dtype)


def stick_breaking(qkv, *, n_heads, q_col, k_col, v_col, tq=256):
    t_len = qkv.shape[0]
    hg = SB_HEADS_PER_STEP
    width = hg * HEAD_DIM
    assert n_heads % hg == 0 and q_col % hg == 0 and k_col % hg == 0 and v_col % hg == 0
    return pl.pallas_call(
        functools.partial(_sb_kernel, tq=tq),
        out_shape=jax.ShapeDtypeStruct((t_len, n_heads * HEAD_DIM), BF16),
        grid=(n_heads // hg, t_len // tq),
        in_specs=[pl.BlockSpec((tq, width), lambda h, i: (i, q_col // hg + h)),
                  pl.BlockSpec((t_len, width), lambda h, i: (0, k_col // hg + h)),
                  pl.BlockSpec((t_len, width), lambda h, i: (0, v_col // hg + h))],
        out_specs=pl.BlockSpec((tq, width), lambda h, i: (i, h)),
        scratch_shapes=[pltpu.VMEM((tq, width), F32), pltpu.VMEM((hg, tq, 1), F32)],
        compiler_params=_params(2),
        name="stick_breaking",
    )(qkv, qkv, qkv)


def _layer_norm(y, g, b):
    mu = jnp.mean(y, axis=-1, keepdims=True)
    yc = y - mu
    var = jnp.mean(yc * yc, axis=-1, keepdims=True)
    return yc * lax.rsqrt(var + LN_EPS) * g + b


def _mixout_kernel(main_ref, qm_ref, mkv_ref, wout_ref, h_ref, g_ref, b_ref, rw2_ref, rb_ref,
                   o_ref, logit_ref, *, alpha):
    main_w = main_ref.shape[1]
    mix = jnp.dot(main_ref[...], wout_ref[0:main_w, :], preferred_element_type=F32)
    q = qm_ref[...].astype(BF16)
    heads = []
    for hd in range(MEM_HEADS):
        lo, hi = hd * HEAD_DIM, (hd + 1) * HEAD_DIM
        kh = mkv_ref[:, lo:hi]
        vh = mkv_ref[:, MEM_WIDTH + lo:MEM_WIDTH + hi]
        s = lax.dot_general(q[:, lo:hi], kh, (((1,), (1,)), ((), ())),
                            preferred_element_type=F32) * (HEAD_DIM ** -0.5)
        e = jnp.exp(s - jnp.max(s, axis=-1, keepdims=True))
        p = e * (1.0 / jnp.sum(e, axis=-1, keepdims=True))
        heads.append(jnp.dot(p.astype(BF16), vh, preferred_element_type=F32).astype(BF16))
    mem_out = jnp.concatenate(heads, axis=-1)
    mix = mix + jnp.dot(mem_out, wout_ref[main_w:, :], preferred_element_type=F32)
    h1 = _layer_norm(alpha * h_ref[...] + mix, g_ref[...], b_ref[...])
    o_ref[...] = h1
    hh = h1.astype(BF16)
    hl = (h1 - hh.astype(F32)).astype(BF16)
    both = jnp.dot(hh, rw2_ref[...], preferred_element_type=F32)
    logit_ref[...] = (both[:, :LANES] + both[:, LANES:]
                      + jnp.dot(hl, rw2_ref[:, :LANES], preferred_element_type=F32) + rb_ref[...])


def mixout(main, qsrc, q_col, mem_kv, w_out, h, ln_g, ln_b, router_w, router_b, *, alpha, tm=512):
    t_len, d = h.shape
    main_w = main.shape[1]
    n_e = router_w.shape[1]
    rw = jnp.zeros((d, LANES), F32).at[:, :n_e].set(router_w)
    rwh = rw.astype(BF16)
    rw2 = jnp.concatenate([rwh, (rw - rwh.astype(F32)).astype(BF16)], axis=1)
    rb = jnp.full((1, LANES), -jnp.inf, F32).at[0, :n_e].set(router_b)
    full = lambda a: pl.BlockSpec(a.shape, lambda i: (0, 0))
    args = (main, qsrc, mem_kv, w_out.astype(BF16), h, ln_g.reshape(1, d), ln_b.reshape(1, d), rw2, rb)
    return pl.pallas_call(
        functools.partial(_mixout_kernel, alpha=alpha),
        out_shape=(jax.ShapeDtypeStruct((t_len, d), F32), jax.ShapeDtypeStruct((t_len, LANES), F32)),
        grid=(t_len // tm,),
        in_specs=[pl.BlockSpec((tm, main_w), lambda i: (i, 0)),
                  pl.BlockSpec((tm, MEM_WIDTH), lambda i: (i, q_col)),
                  full(mem_kv), full(args[3]),
                  pl.BlockSpec((tm, d), lambda i: (i, 0)),
                  full(args[5]), full(args[6]), full(rw2), full(rb)],
        out_specs=(pl.BlockSpec((tm, d), lambda i: (i, 0)), pl.BlockSpec((tm, LANES), lambda i: (i, 0))),
        compiler_params=_params(1),
        name="mixout",
    )(*args)


def _router_kernel(logit_ref, route_ref, cnt_ref, carry_ref, *, tb):
    @pl.when(pl.program_id(0) == 0)
    def _():
        carry_ref[...] = jnp.zeros_like(carry_ref)

    vals = logit_ref[...]
    lane = lax.broadcasted_iota(I32, (tb, LANES), 1).astype(F32)
    sels, tops, ids = [], [], []
    for _ in range(TOP_K):
        m = jnp.max(vals, axis=-1, keepdims=True)
        idx = jnp.min(jnp.where(vals == m, lane, float(LANES)), axis=-1, keepdims=True)
        sel = lane == idx
        vals = jnp.where(sel, -jnp.inf, vals)
        sels.append(sel)
        tops.append(m)
        ids.append(idx)
    exps = [jnp.exp(m - tops[0]) for m in tops]
    inv = 1.0 / (exps[0] + exps[1] + exps[2] + exps[3])
    onehot = jnp.zeros((tb, LANES), F32)
    for sel in sels:
        onehot = onehot + sel.astype(F32)
    rows = lax.broadcasted_iota(I32, (tb, tb), 0)
    cols = lax.broadcasted_iota(I32, (tb, tb), 1)
    before = (cols < rows).astype(BF16)
    cum = jnp.dot(before, onehot.astype(BF16), preferred_element_type=F32) + carry_ref[...]
    out = jnp.zeros((tb, LANES), F32)
    for k in range(TOP_K):
        pos = jnp.sum(jnp.where(sels[k], cum, 0.0), axis=-1, keepdims=True)
        out = jnp.where(lane == k, ids[k], out)
        out = jnp.where(lane == TOP_K + k, exps[k] * inv, out)
        out = jnp.where(lane == 2 * TOP_K + k, pos, out)
    route_ref[...] = out
    carry_ref[...] += jnp.sum(onehot, axis=0, keepdims=True)
    cnt_ref[...] = jnp.broadcast_to(carry_ref[...], cnt_ref.shape)


def router(logits, *, tb=512):
    t_len = logits.shape[0]
    return pl.pallas_call(
        functools.partial(_router_kernel, tb=tb),
        out_shape=(jax.ShapeDtypeStruct((t_len, LANES), F32), jax.ShapeDtypeStruct((SUBLANES, LANES), F32)),
        grid=(t_len // tb,),
        in_specs=[pl.BlockSpec((tb, LANES), lambda i: (i, 0))],
        out_specs=(pl.BlockSpec((tb, LANES), lambda i: (i, 0)), pl.BlockSpec((SUBLANES, LANES), lambda i: (0, 0))),
        scratch_shapes=[pltpu.VMEM((1, LANES), F32)],
        compiler_params=_params(1),
        name="router",
    )(logits)


def _dispatch_kernel(te_ref, na_ref, dest_ref, h_ref, xs_ref, zero_ref, sem, zsem, *, tb, tm, n_tiles):
    @pl.when(pl.program_id(0) == 0)
    def _():
        zero_ref[...] = jnp.zeros_like(zero_ref)

        def has_padding(j):
            return jnp.logical_or(j >= na_ref[0] - 1, te_ref[j] != te_ref[jnp.minimum(j + 1, n_tiles - 1)])

        def zero_copy(j):
            return pltpu.make_async_copy(zero_ref, xs_ref.at[pl.ds(pl.multiple_of(j * tm, tm), tm), :], zsem)

        def zstart(j, c):
            @pl.when(has_padding(j))
            def _():
                zero_copy(j).start()
            return c

        def zwait(j, c):
            @pl.when(has_padding(j))
            def _():
                zero_copy(j).wait()
            return c

        lax.fori_loop(0, n_tiles, zstart, 0)
        lax.fori_loop(0, n_tiles, zwait, 0)

    def start(r, c):
        for k in range(TOP_K):
            slot = dest_ref[0, 0, r * TOP_K + k]
            pltpu.make_async_copy(h_ref.at[pl.ds(r, 1), :], xs_ref.at[pl.ds(slot, 1), :], sem).start()
        return c

    lax.fori_loop(0, tb, start, 0)
    rows = xs_ref.at[pl.ds(0, tb * TOP_K), :]
    pltpu.make_async_copy(rows, rows, sem).wait()


def dispatch(h, dest, tile_e, n_active, n_rows, *, tm, tb=256):
    t_len, d = h.shape
    dest3 = dest.reshape(t_len // tb, 1, tb * TOP_K)
    return pl.pallas_call(
        functools.partial(_dispatch_kernel, tb=tb, tm=tm, n_tiles=n_rows // tm),
        out_shape=jax.ShapeDtypeStruct((n_rows, d), F32),
        grid_spec=pltpu.PrefetchScalarGridSpec(
            num_scalar_prefetch=2,
            grid=(t_len // tb,),
            in_specs=[pl.BlockSpec((1, 1, tb * TOP_K), lambda i, te, na: (i, 0, 0), memory_space=pltpu.SMEM),
                      pl.BlockSpec((tb, d), lambda i, te, na: (i, 0))],
            out_specs=pl.BlockSpec(memory_space=pl.ANY),
            scratch_shapes=[pltpu.VMEM((tm, d), F32), pltpu.SemaphoreType.DMA(()), pltpu.SemaphoreType.DMA(())]),
        compiler_params=_params(1),
        name="dispatch",
    )(tile_e, n_active, dest3, h)


def _tile_idx(i, te_ref, na_ref):
    return jnp.minimum(i, na_ref[0] - 1)


def _zero_inactive(i, na_ref, o_ref):
    @pl.when(i >= na_ref[0])
    def _():
        o_ref[...] = jnp.zeros_like(o_ref)


def _expert_changed(i, te_ref):
    return jnp.logical_or(i == 0, te_ref[i] != te_ref[jnp.maximum(i - 1, 0)])


def _gateup_kernel(te_ref, na_ref, nx_ref, x_ref, w_hbm, b_ref, o_ref, stage_ref, wb_ref, xb_ref, sem):
    i = pl.program_id(0)
    nh = stage_ref.shape[1]
    _zero_inactive(i, na_ref, o_ref)

    def wcopy(e, hf):
        return pltpu.make_async_copy(w_hbm.at[e, :, pl.ds(hf * nh, nh)], stage_ref, sem)

    def reorder(hf):
        rows = lax.broadcasted_iota(I32, (GU, GU), 0)
        cols = lax.broadcasted_iota(I32, (GU, GU), 1)
        src = jnp.where(cols < LANES, 2 * cols, 2 * (cols - LANES) + 1)
        perm = (rows == src).astype(BF16)
        for n in range(0, nh, GU):
            wb_ref[:, hf * nh + n:hf * nh + n + GU] = jnp.dot(
                stage_ref[:, n:n + GU].astype(BF16), perm, preferred_element_type=F32).astype(BF16)

    def columns(hf):
        x = xb_ref[...]
        for n in range(hf * nh, (hf + 1) * nh, GU):
            gu = jnp.dot(x, wb_ref[:, n:n + GU], preferred_element_type=F32) + b_ref[:, n:n + GU]
            gate = jnp.minimum(gu[:, :LANES], SWIGLU_LIMIT)
            up = jnp.clip(gu[:, LANES:], -SWIGLU_LIMIT, SWIGLU_LIMIT)
            act = (up + 1.0) * gate * jax.nn.sigmoid(SWIGLU_ALPHA * gate)
            o_ref[:, n // 2:n // 2 + LANES] = act.astype(o_ref.dtype)

    @pl.when(i < na_ref[0])
    def _():
        changed = _expert_changed(i, te_ref)
        e = te_ref[i]
        xb_ref[...] = x_ref[...].astype(BF16)

        @pl.when(i == 0)
        def _():
            wcopy(e, 0).start()

        @pl.when(changed)
        def _():
            wcopy(e, 0).wait()
            reorder(0)
            wcopy(e, 1).start()

        columns(0)

        @pl.when(changed)
        def _():
            wcopy(e, 1).wait()
            reorder(1)

            @pl.when(nx_ref[i] >= 0)
            def _():
                wcopy(nx_ref[i], 0).start()

        columns(1)


def moe_gateup(tile_e, n_active, next_e, x, w, b, *, tm):
    n_rows, k = x.shape
    n2 = w.shape[2]
    return pl.pallas_call(
        _gateup_kernel,
        out_shape=jax.ShapeDtypeStruct((n_rows, n2 // 2), BF16),
        grid_spec=pltpu.PrefetchScalarGridSpec(
            num_scalar_prefetch=3,
            grid=(n_rows // tm,),
            in_specs=[pl.BlockSpec((tm, k), lambda i, te, na, nx: (_tile_idx(i, te, na), 0)),
                      pl.BlockSpec(memory_space=pl.ANY),
                      pl.BlockSpec((None, 1, n2), lambda i, te, na, nx: (te[_tile_idx(i, te, na)], 0, 0))],
            out_specs=pl.BlockSpec((tm, n2 // 2), lambda i, te, na, nx: (i, 0)),
            scratch_shapes=[pltpu.VMEM((k, n2 // 2), F32), pltpu.VMEM((k, n2), BF16),
                            pltpu.VMEM((tm, k), BF16), pltpu.SemaphoreType.DMA(())]),
        compiler_params=_params(1),
        name="moe_gateup",
    )(tile_e, n_active, next_e, x, w, b)


def _down_kernel(te_ref, na_ref, nx_ref, x_ref, w_hbm, b_ref, o_ref, stage_ref, wb_ref, sem, *, tn):
    i = pl.program_id(0)
    _zero_inactive(i, na_ref, o_ref)

    def wcopy(e):
        return pltpu.make_async_copy(w_hbm.at[e], stage_ref, sem)

    @pl.when(i < na_ref[0])
    def _():
        e = te_ref[i]

        @pl.when(i == 0)
        def _():
            wcopy(e).start()

        @pl.when(_expert_changed(i, te_ref))
        def _():
            wcopy(e).wait()
            wb_ref[...] = stage_ref[...].astype(BF16)

            @pl.when(nx_ref[i] >= 0)
            def _():
                wcopy(nx_ref[i]).start()

        x = x_ref[...]
        for n in range(0, wb_ref.shape[1], tn):
            o_ref[:, n:n + tn] = jnp.dot(x, wb_ref[:, n:n + tn], preferred_element_type=F32) + b_ref[:, n:n + tn]


def moe_down(tile_e, n_active, next_e, x, w, b, *, tm, tn=256):
    n_rows, k = x.shape
    n_out = w.shape[2]
    return pl.pallas_call(
        functools.partial(_down_kernel, tn=tn),
        out_shape=jax.ShapeDtypeStruct((n_rows, n_out), F32),
        grid_spec=pltpu.PrefetchScalarGridSpec(
            num_scalar_prefetch=3,
            grid=(n_rows // tm,),
            in_specs=[pl.BlockSpec((tm, k), lambda i, te, na, nx: (_tile_idx(i, te, na), 0)),
                      pl.BlockSpec(memory_space=pl.ANY),
                      pl.BlockSpec((None, 1, n_out), lambda i, te, na, nx: (te[_tile_idx(i, te, na)], 0, 0))],
            out_specs=pl.BlockSpec((tm, n_out), lambda i, te, na, nx: (i, 0)),
            scratch_shapes=[pltpu.VMEM((k, n_out), F32), pltpu.VMEM((k, n_out), BF16),
                            pltpu.SemaphoreType.DMA(())]),
        compiler_params=_params(1),
        name="moe_down",
    )(tile_e, n_active, next_e, x, w, b)


def _combine_kernel(dest_ref, next_ref, gate_ref, h_ref, g_ref, b_ref, y_ref, o_ref, buf_ref, sems, *, tb, alpha):
    i = pl.program_id(0)
    slot = i % 2

    def gather(idx_ref, s):
        def start(r, c):
            for k in range(TOP_K):
                row = idx_ref[0, 0, r * TOP_K + k]
                pltpu.make_async_copy(y_ref.at[pl.ds(row, 1), :], buf_ref.at[s, k, pl.ds(r, 1), :],
                                      sems.at[s]).start()
            return c
        lax.fori_loop(0, tb, start, 0)

    @pl.when(i == 0)
    def _():
        gather(dest_ref, 0)

    @pl.when(i + 1 < pl.num_programs(0))
    def _():
        gather(next_ref, 1 - slot)

    pltpu.make_async_copy(buf_ref.at[slot], buf_ref.at[slot], sems.at[slot]).wait()
    gates = gate_ref[...]
    ffn = buf_ref[slot, 0] * gates[:, TOP_K:TOP_K + 1]
    for k in range(1, TOP_K):
        ffn = ffn + buf_ref[slot, k] * gates[:, TOP_K + k:TOP_K + k + 1]
    o_ref[...] = _layer_norm(alpha * h_ref[...] + ffn, g_ref[...], b_ref[...])


def combine(y, dest, route, h, ln_g, ln_b, *, alpha, tb=256):
    t_len, d = h.shape
    nb = t_len // tb
    dest3 = dest.reshape(nb, 1, tb * TOP_K)
    idx_spec = lambda f: pl.BlockSpec((1, 1, tb * TOP_K), f, memory_space=pltpu.SMEM)
    return pl.pallas_call(
        functools.partial(_combine_kernel, tb=tb, alpha=alpha),
        out_shape=jax.ShapeDtypeStruct((t_len, d), F32),
        grid=(nb,),
        in_specs=[idx_spec(lambda i: (i, 0, 0)),
                  idx_spec(lambda i: (jnp.minimum(i + 1, nb - 1), 0, 0)),
                  pl.BlockSpec((tb, LANES), lambda i: (i, 0)),
                  pl.BlockSpec((tb, d), lambda i: (i, 0)),
                  pl.BlockSpec((1, d), lambda i: (0, 0)),
                  pl.BlockSpec((1, d), lambda i: (0, 0)),
                  pl.BlockSpec(memory_space=pl.ANY)],
        out_specs=pl.BlockSpec((tb, d), lambda i: (i, 0)),
        scratch_shapes=[pltpu.VMEM((2, TOP_K, tb, d), F32), pltpu.SemaphoreType.DMA((2,))],
        compiler_params=_params(1),
        name="combine",
    )(dest3, dest3, route, h, ln_g.reshape(1, d), ln_b.reshape(1, d), y)


def moe(h1, logits, layer, w_gate_up, b_gate_up, w_down, b_down, ln_g, ln_b, *, alpha, tm=512):
    t_len, d = h1.shape
    n_e = w_gate_up.shape[1]
    route, cnt = router(logits)
    eid = route[:, 0:TOP_K].astype(I32)
    pos = route[:, 2 * TOP_K:3 * TOP_K].astype(I32)
    counts = cnt[0, :n_e].astype(I32)
    padded = (counts + tm - 1) // tm * tm
    pend = jnp.cumsum(padded)
    pstart = pend - padded
    onehot = eid[:, :, None] == jnp.arange(n_e, dtype=I32)[None, None, :]
    dest = jnp.sum(jnp.where(onehot, pstart[None, None, :], 0), axis=-1) + pos
    n_tiles = t_len * TOP_K // tm + n_e
    tile_start = jnp.arange(n_tiles, dtype=I32) * tm
    tile_e = jnp.minimum(jnp.sum((pend[None, :] <= tile_start[:, None]).astype(I32), axis=1), n_e - 1)
    n_active = (pend[-1:] // tm).astype(I32)

    xs = dispatch(h1, dest, tile_e, n_active, n_tiles * tm, tm=tm)
    bgu = b_gate_up.reshape(-1, n_e, b_gate_up.shape[-1] // GU, LANES, 2).swapaxes(-1, -2)
    bgu = bgu.reshape(-1, 1, b_gate_up.shape[-1])
    tile_w = tile_e + layer * n_e
    tidx = jnp.arange(n_tiles, dtype=I32)
    follows = ((tile_w[None, :] != tile_w[:, None]) & (tidx[None, :] > tidx[:, None])
               & (tidx[None, :] < n_active[0]))
    next_e = jnp.where(jnp.any(follows, axis=1), tile_w[jnp.argmax(follows, axis=1)], -1).astype(I32)
    act = moe_gateup(tile_w, n_active, next_e, xs, w_gate_up.reshape((-1,) + w_gate_up.shape[2:]), bgu, tm=tm)
    y = moe_down(tile_w, n_active, next_e, act, w_down.reshape((-1,) + w_down.shape[2:]),
                 b_down.reshape(-1, 1, b_down.shape[-1]), tm=tm)
    return combine(y, dest, route, h1, ln_g, ln_b, alpha=alpha)


def kernel(x, mem, a_w_in, a_conv_w, a_conv_b, a_rg_w, a_rg_b, a_ig_w, a_ig_b, a_lambda, a_w_out, b_w_q, b_w_out, w_kv_shared, mem_w_kv, ln1_g, ln1_b, ln2_g, ln2_b, router_w, router_b, w_gate_up, b_gate_up, w_down, b_down):
    bsz, seq, d = x.shape
    depth = ln1_g.shape[0]
    n_a = a_w_in.shape[0]
    alpha = float((2 * depth) ** 0.25)
    outs = []
    for bi in range(bsz):
        h = x[bi]
        mem_b = mem[bi]
        qkv = None
        for layer in range(depth):
            mem_kv = dense(mem_b, mem_w_kv[layer].astype(BF16), BF16, tm=mem_b.shape[0])
            if layer < n_a:
                lw = a_rg_w.shape[1] * LANES
                proj = dense(h, a_w_in[layer].astype(BF16), F32, tm=512)
                main = rglru(proj, a_conv_w[layer], a_conv_b[layer], a_rg_w[layer], a_rg_b[layer],
                             a_ig_w[layer], a_ig_b[layer], a_lambda[layer])
                qsrc, q_col, w_out = proj, 2 * lw // MEM_WIDTH, a_w_out[layer]
            else:
                j = layer - n_a
                sbw = w_kv_shared.shape[1] // 2
                n_heads = sbw // HEAD_DIM
                if j == 0:
                    w_cat = jnp.concatenate([w_kv_shared, b_w_q[j]], axis=1).astype(BF16)
                    qkv = dense(h, w_cat, BF16, tm=512)
                else:
                    q_only = dense(h, b_w_q[j].astype(BF16), BF16, tm=512)
                    qkv = jnp.concatenate([qkv[:, :2 * sbw], q_only], axis=1)
                main = stick_breaking(qkv, n_heads=n_heads, q_col=2 * n_heads, k_col=0, v_col=n_heads)
                qsrc, q_col, w_out = qkv, (3 * sbw) // MEM_WIDTH, b_w_out[j]
            h1, logits = mixout(main, qsrc, q_col, mem_kv, w_out, h, ln1_g[layer], ln1_b[layer],
                                router_w[layer], router_b[layer], alpha=alpha)
            h = moe(h1, logits, layer, w_gate_up, b_gate_up, w_down, b_down,
                    ln2_g[layer], ln2_b[layer], alpha=alpha)
        outs.append(h)
    return jnp.stack(outs, axis=0)
```

```python
import functools

import jax
import jax.numpy as jnp
from jax import lax
from jax.experimental import pallas as pl
from jax.experimental.pallas import tpu as pltpu

F32 = jnp.float32
BF16 = jnp.bfloat16
I32 = jnp.int32

HEAD_DIM = 128
MEM_HEADS = 4
MEM_WIDTH = MEM_HEADS * HEAD_DIM
CONV_WIDTH = 4
LRU_C = 8.0
N_EXPERTS = 32
TOP_K = 4
SWIGLU_LIMIT = 7.0
SWIGLU_ALPHA = 1.702
LN_EPS = 1e-5
LANES = 128
SUBLANES = 8
GU = 2 * LANES
VMEM_LIMIT = 56 * 1024 * 1024
SB_SKIP = 100.0
SB_HEADS_PER_STEP = 2


def _params(n_axes, vmem=VMEM_LIMIT):
    return pltpu.CompilerParams(dimension_semantics=("arbitrary",) * n_axes, vmem_limit_bytes=vmem)


def _dense_kernel(x_ref, w_ref, o_ref, *, tn):
    xb = x_ref[...].astype(BF16)
    for n in range(0, w_ref.shape[1], tn):
        o_ref[:, n:n + tn] = jnp.dot(xb, w_ref[:, n:n + tn], preferred_element_type=F32).astype(o_ref.dtype)


def dense(x, w, out_dtype, *, tm, tn=256):
    m, k = x.shape
    n = w.shape[1]
    return pl.pallas_call(
        functools.partial(_dense_kernel, tn=tn),
        out_shape=jax.ShapeDtypeStruct((m, n), out_dtype),
        grid=(m // tm,),
        in_specs=[pl.BlockSpec((tm, k), lambda i: (i, 0)),
                  pl.BlockSpec((k, n), lambda i: (0, 0), pipeline_mode=pl.Buffered(1))],
        out_specs=pl.BlockSpec((tm, n), lambda i: (i, 0)),
        compiler_params=_params(1),
        name="dense",
    )(x, w)


def _softplus(x):
    return jnp.maximum(x, 0.0) + jnp.log1p(jnp.exp(-jnp.abs(x)))


def _gelu_tanh(x):
    return 0.5 * x * (1.0 + jnp.tanh(0.7978845608028654 * (x + 0.044715 * (x * x * x))))


def _rglru_kernel(gate_ref, u_ref, cw_ref, cb_ref, rgw_ref, rgb_ref, igw_ref, igb_ref, lam_ref,
                  o_ref, hist_ref, carry_ref, *, tt):
    t = pl.program_id(1)

    @pl.when(t == 0)
    def _():
        hist_ref[...] = jnp.zeros_like(hist_ref)
        carry_ref[...] = jnp.zeros_like(carry_ref)

    u = u_ref[...]
    ext = jnp.concatenate([hist_ref[...], u], axis=0)
    cw = cw_ref[...]
    xc = cb_ref[...] + u * cw[CONV_WIDTH - 1:CONV_WIDTH, :]
    for d in range(1, CONV_WIDTH):
        shifted = pltpu.roll(ext, d, axis=0)[SUBLANES:, :]
        xc = xc + shifted * cw[CONV_WIDTH - 1 - d:CONV_WIDTH - d, :]
    hist_ref[...] = u[tt - SUBLANES:, :]

    xcb = xc.astype(BF16)
    r = jax.nn.sigmoid(jnp.dot(xcb, rgw_ref[...], preferred_element_type=F32) + rgb_ref[...])
    ig = jax.nn.sigmoid(jnp.dot(xcb, igw_ref[...], preferred_element_type=F32) + igb_ref[...])
    log_a = (-LRU_C) * r * _softplus(-lam_ref[...])
    a = jnp.exp(log_a)
    b = jnp.sqrt(-jnp.tanh(log_a) * (a * a + 1.0)) * (ig * xc)

    groups = tt // SUBLANES
    a = a.reshape(groups, SUBLANES, LANES)
    b = b.reshape(groups, SUBLANES, LANES)
    sub = lax.broadcasted_iota(I32, (groups, SUBLANES, LANES), 1)
    d = 1
    while d < SUBLANES:
        keep = sub >= d
        a_sh = jnp.where(keep, pltpu.roll(a, d, axis=1), 1.0)
        b_sh = jnp.where(keep, pltpu.roll(b, d, axis=1), 0.0)
        b = a * b_sh + b
        a = a * a_sh
        d *= 2
    a = a.reshape(tt, LANES)
    b = b.reshape(tt, LANES)
    carry = carry_ref[...]
    hs = []
    for g in range(tt // SUBLANES):
        rows = slice(g * SUBLANES, (g + 1) * SUBLANES)
        hs.append(a[rows, :] * carry + b[rows, :])
        carry = hs[-1][SUBLANES - 1:SUBLANES, :]
    carry_ref[...] = carry
    o_ref[...] = (jnp.concatenate(hs, axis=0) * _gelu_tanh(gate_ref[...])).astype(o_ref.dtype)


def rglru(proj, conv_w, conv_b, rg_w, rg_b, ig_w, ig_b, lam, *, tt=512):
    t_len = proj.shape[0]
    nblk = rg_w.shape[0]
    width = nblk * LANES
    vec = lambda v: v.reshape(1, width)
    chan = lambda rows: pl.BlockSpec((rows, LANES), lambda n, t: (0, n))
    return pl.pallas_call(
        functools.partial(_rglru_kernel, tt=tt),
        out_shape=jax.ShapeDtypeStruct((t_len, width), BF16),
        grid=(nblk, t_len // tt),
        in_specs=[pl.BlockSpec((tt, LANES), lambda n, t: (t, n)),
                  pl.BlockSpec((tt, LANES), lambda n, t: (t, nblk + n)),
                  chan(CONV_WIDTH), chan(1),
                  pl.BlockSpec((None, LANES, LANES), lambda n, t: (n, 0, 0)), chan(1),
                  pl.BlockSpec((None, LANES, LANES), lambda n, t: (n, 0, 0)), chan(1),
                  chan(1)],
        out_specs=pl.BlockSpec((tt, LANES), lambda n, t: (t, n)),
        scratch_shapes=[pltpu.VMEM((SUBLANES, LANES), F32), pltpu.VMEM((1, LANES), F32)],
        compiler_params=_params(2),
        name="rglru",
    )(proj, proj, conv_w, vec(conv_b), rg_w.astype(BF16), vec(rg_b), ig_w.astype(BF16), vec(ig_b), vec(lam))


def _sb_kernel(q_ref, k_ref, v_ref, o_ref, acc_ref, run_ref, *, tq):
    i = pl.program_id(1)
    scale = HEAD_DIM ** -0.5
    rows = lax.broadcasted_iota(I32, (tq, tq), 0)
    cols = lax.broadcasted_iota(I32, (tq, tq), 1)
    later = (rows > cols).astype(BF16)
    causal = cols < rows
    heads = range(SB_HEADS_PER_STEP)

    def rows_of(j):
        return pl.ds(j * tq if isinstance(j, int) else pl.multiple_of(j * tq, tq), tq)

    def lanes_of(hd):
        return slice(hd * HEAD_DIM, (hd + 1) * HEAD_DIM)

    def front(hd, j, masked):
        z = lax.dot_general(q_ref[:, lanes_of(hd)], k_ref[rows_of(j), lanes_of(hd)], (((1,), (1,)), ((), ())),
                            preferred_element_type=F32) * scale
        sp = jnp.maximum(z, 0.0) + jnp.log(1.0 + jnp.exp(-jnp.abs(z)))
        log_keep = -sp
        log_beta = z - sp
        if masked:
            log_keep = jnp.where(causal, log_keep, 0.0)
        hi = log_keep.astype(BF16)
        lo = (log_keep - hi.astype(F32)).astype(BF16)
        after = (jnp.dot(hi, later, preferred_element_type=F32)
                 + jnp.dot(lo, later, preferred_element_type=F32))
        return log_beta + after, after[:, 0:1] + log_keep[:, 0:1]

    def back(hd, j, logw, run, masked):
        w = jnp.exp(logw if run is None else logw + run)
        if masked:
            w = jnp.where(causal, w, 0.0)
        return jnp.dot(w.astype(BF16), v_ref[rows_of(j), lanes_of(hd)], preferred_element_type=F32)

    @pl.when(i == 0)
    def _():
        for hd in heads:
            logw, _ = front(hd, 0, True)
            o_ref[:, lanes_of(hd)] = back(hd, 0, logw, None, True).astype(o_ref.dtype)

    @pl.when(i > 0)
    def _():
        tops = []
        for hd in heads:
            logw_d, tot_d = front(hd, i, True)
            logw_p, tot_p = front(hd, i - 1, False)
            acc_ref[:, lanes_of(hd)] = back(hd, i, logw_d, None, True) + back(hd, i - 1, logw_p, tot_d, False)
            run0 = tot_d + tot_p
            run_ref[hd] = run0
            tops.append(jnp.max(run0))

        def cond(state):
            j, top = state
            return jnp.logical_and(j >= 0, top > -SB_SKIP)

        def body(state):
            j, _ = state
            tops = []
            for hd in heads:
                logw, tot = front(hd, j, False)
                run = run_ref[hd]
                acc_ref[:, lanes_of(hd)] += back(hd, j, logw, run, False)
                run = run + tot
                run_ref[hd] = run
                tops.append(jnp.max(run))
            return j - 1, functools.reduce(jnp.maximum, tops)

        lax.while_loop(cond, body, (i - 2, functools.reduce(jnp.maximum, tops)))
        o_ref[...] = acc_ref[...].astype(o_ref.dtype)


def stick_breaking(qkv, *, n_heads, q_col, k_col, v_col, tq=256):
    t_len = qkv.shape[0]
    hg = SB_HEADS_PER_STEP
    width = hg * HEAD_DIM
    assert n_heads % hg == 0 and q_col % hg == 0 and k_col % hg == 0 and v_col % hg == 0
    return pl.pallas_call(
        functools.partial(_sb_kernel, tq=tq),
        out_shape=jax.ShapeDtypeStruct((t_len, n_heads * HEAD_DIM), BF16),
        grid=(n_heads // hg, t_len // tq),
        in_specs=[pl.BlockSpec((tq, width), lambda h, i: (i, q_col // hg + h)),
                  pl.BlockSpec((t_len, width), lambda h, i: (0, k_col // hg + h)),
                  pl.BlockSpec((t_len, width), lambda h, i: (0, v_col // hg + h))],
        out_specs=pl.BlockSpec((tq, width), lambda h, i: (i, h)),
        scratch_shapes=[pltpu.VMEM((tq, width), F32), pltpu.VMEM((hg, tq, 1), F32)],
        compiler_params=_params(2),
        name="stick_breaking",
    )(qkv, qkv, qkv)


def _layer_norm(y, g, b):
    mu = jnp.mean(y, axis=-1, keepdims=True)
    yc = y - mu
    var = jnp.mean(yc * yc, axis=-1, keepdims=True)
    return yc * lax.rsqrt(var + LN_EPS) * g + b


def _mixout_kernel(main_ref, qm_ref, mkv_ref, wout_ref, h_ref, g_ref, b_ref, rw2_ref, rb_ref,
                   o_ref, logit_ref, *, alpha):
    main_w = main_ref.shape[1]
    mix = jnp.dot(main_ref[...], wout_ref[0:main_w, :], preferred_element_type=F32)
    q = qm_ref[...].astype(BF16)
    heads = []
    for hd in range(MEM_HEADS):
        lo, hi = hd * HEAD_DIM, (hd + 1) * HEAD_DIM
        kh = mkv_ref[:, lo:hi]
        vh = mkv_ref[:, MEM_WIDTH + lo:MEM_WIDTH + hi]
        s = lax.dot_general(q[:, lo:hi], kh, (((1,), (1,)), ((), ())),
                            preferred_element_type=F32) * (HEAD_DIM ** -0.5)
        e = jnp.exp(s - jnp.max(s, axis=-1, keepdims=True))
        p = e * (1.0 / jnp.sum(e, axis=-1, keepdims=True))
        heads.append(jnp.dot(p.astype(BF16), vh, preferred_element_type=F32).astype(BF16))
    mem_out = jnp.concatenate(heads, axis=-1)
    mix = mix + jnp.dot(mem_out, wout_ref[main_w:, :], preferred_element_type=F32)
    h1 = _layer_norm(alpha * h_ref[...] + mix, g_ref[...], b_ref[...])
    o_ref[...] = h1
    hh = h1.astype(BF16)
    hl = (h1 - hh.astype(F32)).astype(BF16)
    both = jnp.dot(hh, rw2_ref[...], preferred_element_type=F32)
    logit_ref[...] = (both[:, :LANES] + both[:, LANES:]
                      + jnp.dot(hl, rw2_ref[:, :LANES], preferred_element_type=F32) + rb_ref[...])


def mixout(main, qsrc, q_col, mem_kv, w_out, h, ln_g, ln_b, router_w, router_b, *, alpha, tm=512):
    t_len, d = h.shape
    main_w = main.shape[1]
    n_e = router_w.shape[1]
    rw = jnp.zeros((d, LANES), F32).at[:, :n_e].set(router_w)
    rwh = rw.astype(BF16)
    rw2 = jnp.concatenate([rwh, (rw - rwh.astype(F32)).astype(BF16)], axis=1)
    rb = jnp.full((1, LANES), -jnp.inf, F32).at[0, :n_e].set(router_b)
    full = lambda a: pl.BlockSpec(a.shape, lambda i: (0, 0))
    args = (main, qsrc, mem_kv, w_out.astype(BF16), h, ln_g.reshape(1, d), ln_b.reshape(1, d), rw2, rb)
    return pl.pallas_call(
        functools.partial(_mixout_kernel, alpha=alpha),
        out_shape=(jax.ShapeDtypeStruct((t_len, d), F32), jax.ShapeDtypeStruct((t_len, LANES), F32)),
        grid=(t_len // tm,),
        in_specs=[pl.BlockSpec((tm, main_w), lambda i: (i, 0)),
                  pl.BlockSpec((tm, MEM_WIDTH), lambda i: (i, q_col)),
                  full(mem_kv), full(args[3]),
                  pl.BlockSpec((tm, d), lambda i: (i, 0)),
                  full(args[5]), full(args[6]), full(rw2), full(rb)],
        out_specs=(pl.BlockSpec((tm, d), lambda i: (i, 0)), pl.BlockSpec((tm, LANES), lambda i: (i, 0))),
        compiler_params=_params(1),
        name="mixout",
    )(*args)


def _router_kernel(logit_ref, route_ref, cnt_ref, carry_ref, *, tb):
    @pl.when(pl.program_id(0) == 0)
    def _():
        carry_ref[...] = jnp.zeros_like(carry_ref)

    vals = logit_ref[...]
    lane = lax.broadcasted_iota(I32, (tb, LANES), 1).astype(F32)
    sels, tops, ids = [], [], []
    for _ in range(TOP_K):
        m = jnp.max(vals, axis=-1, keepdims=True)
        idx = jnp.min(jnp.where(vals == m, lane, float(LANES)), axis=-1, keepdims=True)
        sel = lane == idx
        vals = jnp.where(sel, -jnp.inf, vals)
        sels.append(sel)
        tops.append(m)
        ids.append(idx)
    exps = [jnp.exp(m - tops[0]) for m in tops]
    inv = 1.0 / (exps[0] + exps[1] + exps[2] + exps[3])
    onehot = jnp.zeros((tb, LANES), F32)
    for sel in sels:
        onehot = onehot + sel.astype(F32)
    rows = lax.broadcasted_iota(I32, (tb, tb), 0)
    cols = lax.broadcasted_iota(I32, (tb, tb), 1)
    before = (cols < rows).astype(BF16)
    cum = jnp.dot(before, onehot.astype(BF16), preferred_element_type=F32) + carry_ref[...]
    out = jnp.zeros((tb, LANES), F32)
    for k in range(TOP_K):
        pos = jnp.sum(jnp.where(sels[k], cum, 0.0), axis=-1, keepdims=True)
        out = jnp.where(lane == k, ids[k], out)
        out = jnp.where(lane == TOP_K + k, exps[k] * inv, out)
        out = jnp.where(lane == 2 * TOP_K + k, pos, out)
    route_ref[...] = out
    carry_ref[...] += jnp.sum(onehot, axis=0, keepdims=True)
    cnt_ref[...] = jnp.broadcast_to(carry_ref[...], cnt_ref.shape)


def router(logits, *, tb=512):
    t_len = logits.shape[0]
    return pl.pallas_call(
        functools.partial(_router_kernel, tb=tb),
        out_shape=(jax.ShapeDtypeStruct((t_len, LANES), F32), jax.ShapeDtypeStruct((SUBLANES, LANES), F32)),
        grid=(t_len // tb,),
        in_specs=[pl.BlockSpec((tb, LANES), lambda i: (i, 0))],
        out_specs=(pl.BlockSpec((tb, LANES), lambda i: (i, 0)), pl.BlockSpec((SUBLANES, LANES), lambda i: (0, 0))),
        scratch_shapes=[pltpu.VMEM((1, LANES), F32)],
        compiler_params=_params(1),
        name="router",
    )(logits)


def _dispatch_kernel(te_ref, na_ref, dest_ref, h_ref, xs_ref, zero_ref, sem, zsem, *, tb, tm, n_tiles):
    @pl.when(pl.program_id(0) == 0)
    def _():
        zero_ref[...] = jnp.zeros_like(zero_ref)

        def has_padding(j):
            return jnp.logical_or(j >= na_ref[0] - 1, te_ref[j] != te_ref[jnp.minimum(j + 1, n_tiles - 1)])

        def zero_copy(j):
            return pltpu.make_async_copy(zero_ref, xs_ref.at[pl.ds(pl.multiple_of(j * tm, tm), tm), :], zsem)

        def zstart(j, c):
            @pl.when(has_padding(j))
            def _():
                zero_copy(j).start()
            return c

        def zwait(j, c):
            @pl.when(has_padding(j))
            def _():
                zero_copy(j).wait()
            return c

        lax.fori_loop(0, n_tiles, zstart, 0)
        lax.fori_loop(0, n_tiles, zwait, 0)

    def start(r, c):
        for k in range(TOP_K):
            slot = dest_ref[0, 0, r * TOP_K + k]
            pltpu.make_async_copy(h_ref.at[pl.ds(r, 1), :], xs_ref.at[pl.ds(slot, 1), :], sem).start()
        return c

    lax.fori_loop(0, tb, start, 0)
    rows = xs_ref.at[pl.ds(0, tb * TOP_K), :]
    pltpu.make_async_copy(rows, rows, sem).wait()


def dispatch(h, dest, tile_e, n_active, n_rows, *, tm, tb=256):
    t_len, d = h.shape
    dest3 = dest.reshape(t_len // tb, 1, tb * TOP_K)
    return pl.pallas_call(
        functools.partial(_dispatch_kernel, tb=tb, tm=tm, n_tiles=n_rows // tm),
        out_shape=jax.ShapeDtypeStruct((n_rows, d), F32),
        grid_spec=pltpu.PrefetchScalarGridSpec(
            num_scalar_prefetch=2,
            grid=(t_len // tb,),
            in_specs=[pl.BlockSpec((1, 1, tb * TOP_K), lambda i, te, na: (i, 0, 0), memory_space=pltpu.SMEM),
                      pl.BlockSpec((tb, d), lambda i, te, na: (i, 0))],
            out_specs=pl.BlockSpec(memory_space=pl.ANY),
            scratch_shapes=[pltpu.VMEM((tm, d), F32), pltpu.SemaphoreType.DMA(()), pltpu.SemaphoreType.DMA(())]),
        compiler_params=_params(1),
        name="dispatch",
    )(tile_e, n_active, dest3, h)


def _tile_idx(i, te_ref, na_ref):
    return jnp.minimum(i, na_ref[0] - 1)


def _zero_inactive(i, na_ref, o_ref):
    @pl.when(i >= na_ref[0])
    def _():
        o_ref[...] = jnp.zeros_like(o_ref)


def _expert_changed(i, te_ref):
    return jnp.logical_or(i == 0, te_ref[i] != te_ref[jnp.maximum(i - 1, 0)])


def _gateup_kernel(te_ref, na_ref, nx_ref, nv_ref, x_ref, w_hbm, b_ref, o_ref, stage_ref, wb_ref, xb_ref, sem):
    i = pl.program_id(0)
    nh = stage_ref.shape[1]
    _zero_inactive(i, na_ref, o_ref)

    def wcopy(e, hf):
        return pltpu.make_async_copy(w_hbm.at[e, :, pl.ds(hf * nh, nh)], stage_ref, sem)

    def reorder(hf):
        rows = lax.broadcasted_iota(I32, (GU, GU), 0)
        cols = lax.broadcasted_iota(I32, (GU, GU), 1)
        src = jnp.where(cols < LANES, 2 * cols, 2 * (cols - LANES) + 1)
        perm = (rows == src).astype(BF16)
        for n in range(0, nh, GU):
            wb_ref[:, hf * nh + n:hf * nh + n + GU] = jnp.dot(
                stage_ref[:, n:n + GU].astype(BF16), perm, preferred_element_type=F32).astype(BF16)

    def columns_of(hf, rows):
        x = xb_ref[0:rows, :]
        for n in range(hf * nh, (hf + 1) * nh, GU):
            gu = jnp.dot(x, wb_ref[:, n:n + GU], preferred_element_type=F32) + b_ref[:, n:n + GU]
            gate = jnp.minimum(gu[:, :LANES], SWIGLU_LIMIT)
            up = jnp.clip(gu[:, LANES:], -SWIGLU_LIMIT, SWIGLU_LIMIT)
            act = (up + 1.0) * gate * jax.nn.sigmoid(SWIGLU_ALPHA * gate)
            o_ref[0:rows, n // 2:n // 2 + LANES] = act.astype(o_ref.dtype)

    def columns(hf):
        tm = o_ref.shape[0]
        upper_used = nv_ref[i] > tm // 2

        @pl.when(upper_used)
        def _():
            columns_of(hf, tm)

        @pl.when(jnp.logical_not(upper_used))
        def _():
            columns_of(hf, tm // 2)
            o_ref[tm // 2:, hf * nh // 2:(hf + 1) * nh // 2] = jnp.zeros((tm - tm // 2, nh // 2), o_ref.dtype)

    @pl.when(i < na_ref[0])
    def _():
        changed = _expert_changed(i, te_ref)
        e = te_ref[i]
        xb_ref[...] = x_ref[...].astype(BF16)

        @pl.when(i == 0)
        def _():
            wcopy(e, 0).start()

        @pl.when(changed)
        def _():
            wcopy(e, 0).wait()
            reorder(0)
            wcopy(e, 1).start()

        columns(0)

        @pl.when(changed)
        def _():
            wcopy(e, 1).wait()
            reorder(1)

            @pl.when(nx_ref[i] >= 0)
            def _():
                wcopy(nx_ref[i], 0).start()

        columns(1)


def moe_gateup(tile_e, n_active, next_e, n_valid, x, w, b, *, tm):
    n_rows, k = x.shape
    n2 = w.shape[2]
    return pl.pallas_call(
        _gateup_kernel,
        out_shape=jax.ShapeDtypeStruct((n_rows, n2 // 2), BF16),
        grid_spec=pltpu.PrefetchScalarGridSpec(
            num_scalar_prefetch=4,
            grid=(n_rows // tm,),
            in_specs=[pl.BlockSpec((tm, k), lambda i, te, na, nx, nv: (_tile_idx(i, te, na), 0)),
                      pl.BlockSpec(memory_space=pl.ANY),
                      pl.BlockSpec((None, 1, n2), lambda i, te, na, nx, nv: (te[_tile_idx(i, te, na)], 0, 0))],
            out_specs=pl.BlockSpec((tm, n2 // 2), lambda i, te, na, nx, nv: (i, 0)),
            scratch_shapes=[pltpu.VMEM((k, n2 // 2), F32), pltpu.VMEM((k, n2), BF16),
                            pltpu.VMEM((tm, k), BF16), pltpu.SemaphoreType.DMA(())]),
        compiler_params=_params(1),
        name="moe_gateup",
    )(tile_e, n_active, next_e, n_valid, x, w, b)


def _down_kernel(te_ref, na_ref, nx_ref, nv_ref, x_ref, w_hbm, b_ref, o_ref, stage_ref, wb_ref, sem, *, tn):
    i = pl.program_id(0)
    _zero_inactive(i, na_ref, o_ref)

    def wcopy(e):
        return pltpu.make_async_copy(w_hbm.at[e], stage_ref, sem)

    @pl.when(i < na_ref[0])
    def _():
        e = te_ref[i]

        @pl.when(i == 0)
        def _():
            wcopy(e).start()

        @pl.when(_expert_changed(i, te_ref))
        def _():
            wcopy(e).wait()
            wb_ref[...] = stage_ref[...].astype(BF16)

            @pl.when(nx_ref[i] >= 0)
            def _():
                wcopy(nx_ref[i]).start()

        def rows_of(rows):
            x = x_ref[0:rows, :]
            for n in range(0, wb_ref.shape[1], tn):
                o_ref[0:rows, n:n + tn] = (jnp.dot(x, wb_ref[:, n:n + tn], preferred_element_type=F32)
                                           + b_ref[:, n:n + tn])

        tm = o_ref.shape[0]
        upper_used = nv_ref[i] > tm // 2

        @pl.when(upper_used)
        def _():
            rows_of(tm)

        @pl.when(jnp.logical_not(upper_used))
        def _():
            rows_of(tm // 2)
            o_ref[tm // 2:, :] = jnp.zeros((tm - tm // 2, o_ref.shape[1]), o_ref.dtype)


def moe_down(tile_e, n_active, next_e, n_valid, x, w, b, *, tm, tn=256):
    n_rows, k = x.shape
    n_out = w.shape[2]
    return pl.pallas_call(
        functools.partial(_down_kernel, tn=tn),
        out_shape=jax.ShapeDtypeStruct((n_rows, n_out), F32),
        grid_spec=pltpu.PrefetchScalarGridSpec(
            num_scalar_prefetch=4,
            grid=(n_rows // tm,),
            in_specs=[pl.BlockSpec((tm, k), lambda i, te, na, nx, nv: (_tile_idx(i, te, na), 0)),
                      pl.BlockSpec(memory_space=pl.ANY),
                      pl.BlockSpec((None, 1, n_out), lambda i, te, na, nx, nv: (te[_tile_idx(i, te, na)], 0, 0))],
            out_specs=pl.BlockSpec((tm, n_out), lambda i, te, na, nx, nv: (i, 0)),
            scratch_shapes=[pltpu.VMEM((k, n_out), F32), pltpu.VMEM((k, n_out), BF16),
                            pltpu.SemaphoreType.DMA(())]),
        compiler_params=_params(1),
        name="moe_down",
    )(tile_e, n_active, next_e, n_valid, x, w, b)


def _combine_kernel(dest_ref, next_ref, gate_ref, h_ref, g_ref, b_ref, y_ref, o_ref, buf_ref, sems, *, tb, alpha):
    i = pl.program_id(0)

    def gather(idx_ref, s):
        def start(r, c):
            for k in range(TOP_K):
                row = idx_ref[0, 0, r * TOP_K + k]
                pltpu.make_async_copy(y_ref.at[pl.ds(row, 1), :], buf_ref.at[s, k, pl.ds(r, 1), :],
                                      sems.at[s]).start()
            return c
        lax.fori_loop(0, tb, start, 0)

    @pl.when(i == 0)
    def _():
        gather(dest_ref, 0)

    for s in range(2):
        @pl.when(i % 2 == s)
        def _():
            @pl.when(i + 1 < pl.num_programs(0))
            def _():
                gather(next_ref, 1 - s)

            pltpu.make_async_copy(buf_ref.at[s], buf_ref.at[s], sems.at[s]).wait()
            gates = gate_ref[...]
            ffn = buf_ref[s, 0] * gates[:, TOP_K:TOP_K + 1]
            for k in range(1, TOP_K):
                ffn = ffn + buf_ref[s, k] * gates[:, TOP_K + k:TOP_K + k + 1]
            o_ref[...] = _layer_norm(alpha * h_ref[...] + ffn, g_ref[...], b_ref[...])


def combine(y, dest, route, h, ln_g, ln_b, *, alpha, tb=256):
    t_len, d = h.shape
    nb = t_len // tb
    dest3 = dest.reshape(nb, 1, tb * TOP_K)
    idx_spec = lambda f: pl.BlockSpec((1, 1, tb * TOP_K), f, memory_space=pltpu.SMEM)
    return pl.pallas_call(
        functools.partial(_combine_kernel, tb=tb, alpha=alpha),
        out_shape=jax.ShapeDtypeStruct((t_len, d), F32),
        grid=(nb,),
        in_specs=[idx_spec(lambda i: (i, 0, 0)),
                  idx_spec(lambda i: (jnp.minimum(i + 1, nb - 1), 0, 0)),
                  pl.BlockSpec((tb, LANES), lambda i: (i, 0)),
                  pl.BlockSpec((tb, d), lambda i: (i, 0)),
                  pl.BlockSpec((1, d), lambda i: (0, 0)),
                  pl.BlockSpec((1, d), lambda i: (0, 0)),
                  pl.BlockSpec(memory_space=pl.ANY)],
        out_specs=pl.BlockSpec((tb, d), lambda i: (i, 0)),
        scratch_shapes=[pltpu.VMEM((2, TOP_K, tb, d), F32), pltpu.SemaphoreType.DMA((2,))],
        compiler_params=_params(1),
        name="combine",
    )(dest3, dest3, route, h, ln_g.reshape(1, d), ln_b.reshape(1, d), y)


def moe(h1, logits, layer, w_gate_up, b_gate_up, w_down, b_down, ln_g, ln_b, *, alpha, tm=512):
    t_len, d = h1.shape
    n_e = w_gate_up.shape[1]
    route, cnt = router(logits)
    eid = route[:, 0:TOP_K].astype(I32)
    pos = route[:, 2 * TOP_K:3 * TOP_K].astype(I32)
    counts = cnt[0, :n_e].astype(I32)
    padded = (counts + tm - 1) // tm * tm
    pend = jnp.cumsum(padded)
    pstart = pend - padded
    onehot = eid[:, :, None] == jnp.arange(n_e, dtype=I32)[None, None, :]
    dest = jnp.sum(jnp.where(onehot, pstart[None, None, :], 0), axis=-1) + pos
    n_tiles = t_len * TOP_K // tm + n_e
    tile_start = jnp.arange(n_tiles, dtype=I32) * tm
    tile_e = jnp.minimum(jnp.sum((pend[None, :] <= tile_start[:, None]).astype(I32), axis=1), n_e - 1)
    n_active = (pend[-1:] // tm).astype(I32)

    xs = dispatch(h1, dest, tile_e, n_active, n_tiles * tm, tm=tm)
    bgu = b_gate_up.reshape(-1, n_e, b_gate_up.shape[-1] // GU, LANES, 2).swapaxes(-1, -2)
    bgu = bgu.reshape(-1, 1, b_gate_up.shape[-1])
    tile_w = tile_e + layer * n_e
    tidx = jnp.arange(n_tiles, dtype=I32)
    follows = ((tile_w[None, :] != tile_w[:, None]) & (tidx[None, :] > tidx[:, None])
               & (tidx[None, :] < n_active[0]))
    next_e = jnp.where(jnp.any(follows, axis=1), tile_w[jnp.argmax(follows, axis=1)], -1).astype(I32)
    group_end = (pstart + counts)[tile_e]
    n_valid = jnp.clip(group_end - tile_start, 0, tm).astype(I32)
    act = moe_gateup(tile_w, n_active, next_e, n_valid, xs, w_gate_up.reshape((-1,) + w_gate_up.shape[2:]),
                     bgu, tm=tm)
    y = moe_down(tile_w, n_active, next_e, n_valid, act, w_down.reshape((-1,) + w_down.shape[2:]),
                 b_down.reshape(-1, 1, b_down.shape[-1]), tm=tm)
    return combine(y, dest, route, h1, ln_g, ln_b, alpha=alpha)


def kernel(x, mem, a_w_in, a_conv_w, a_conv_b, a_rg_w, a_rg_b, a_ig_w, a_ig_b, a_lambda, a_w_out, b_w_q, b_w_out, w_kv_shared, mem_w_kv, ln1_g, ln1_b, ln2_g, ln2_b, router_w, router_b, w_gate_up, b_gate_up, w_down, b_down):
    bsz, seq, d = x.shape
    depth = ln1_g.shape[0]
    n_a = a_w_in.shape[0]
    alpha = float((2 * depth) ** 0.25)
    outs = []
    for bi in range(bsz):
        h = x[bi]
        mem_b = mem[bi]
        qkv = None
        for layer in range(depth):
            mem_kv = dense(mem_b, mem_w_kv[layer].astype(BF16), BF16, tm=mem_b.shape[0])
            if layer < n_a:
                lw = a_rg_w.shape[1] * LANES
                proj = dense(h, a_w_in[layer].astype(BF16), F32, tm=512)
                main = rglru(proj, a_conv_w[layer], a_conv_b[layer], a_rg_w[layer], a_rg_b[layer],
                             a_ig_w[layer], a_ig_b[layer], a_lambda[layer])
                qsrc, q_col, w_out = proj, 2 * lw // MEM_WIDTH, a_w_out[layer]
            else:
                j = layer - n_a
                sbw = w_kv_shared.shape[1] // 2
                n_heads = sbw // HEAD_DIM
                if j == 0:
                    w_cat = jnp.concatenate([w_kv_shared, b_w_q[j]], axis=1).astype(BF16)
                    qkv = dense(h, w_cat, BF16, tm=512)
                else:
                    q_only = dense(h, b_w_q[j].astype(BF16), BF16, tm=512)
                    qkv = jnp.concatenate([qkv[:, :2 * sbw], q_only], axis=1)
                main = stick_breaking(qkv, n_heads=n_heads, q_col=2 * n_heads, k_col=0, v_col=n_heads)
                qsrc, q_col, w_out = qkv, (3 * sbw) // MEM_WIDTH, b_w_out[j]
            h1, logits = mixout(main, qsrc, q_col, mem_kv, w_out, h, ln1_g[layer], ln1_b[layer],
                                router_w[layer], router_b[layer], alpha=alpha)
            h = moe(h1, logits, layer, w_gate_up, b_gate_up, w_down, b_down,
                    ln2_g[layer], ln2_b[layer], alpha=alpha)
        outs.append(h)
    return jnp.stack(outs, axis=0)
```

```python
import functools

import jax
import jax.numpy as jnp
from jax import lax
from jax.experimental import pallas as pl
from jax.experimental.pallas import tpu as pltpu

F32 = jnp.float32
BF16 = jnp.bfloat16
I32 = jnp.int32

HEAD_DIM = 128
MEM_HEADS = 4
MEM_WIDTH = MEM_HEADS * HEAD_DIM
CONV_WIDTH = 4
LRU_C = 8.0
N_EXPERTS = 32
TOP_K = 4
SWIGLU_LIMIT = 7.0
SWIGLU_ALPHA = 1.702
LN_EPS = 1e-5
LANES = 128
SUBLANES = 8
GU = 2 * LANES
VMEM_LIMIT = 56 * 1024 * 1024
SB_SKIP = 100.0
SB_HEADS_PER_STEP = 2


def _params(n_axes, vmem=VMEM_LIMIT):
    return pltpu.CompilerParams(dimension_semantics=("arbitrary",) * n_axes, vmem_limit_bytes=vmem)


def _dense_kernel(x_ref, w_ref, o_ref, *, tn):
    xb = x_ref[...].astype(BF16)
    for n in range(0, w_ref.shape[1], tn):
        o_ref[:, n:n + tn] = jnp.dot(xb, w_ref[:, n:n + tn], preferred_element_type=F32).astype(o_ref.dtype)


def dense(x, w, out_dtype, *, tm, tn=256):
    m, k = x.shape
    n = w.shape[1]
    return pl.pallas_call(
        functools.partial(_dense_kernel, tn=tn),
        out_shape=jax.ShapeDtypeStruct((m, n), out_dtype),
        grid=(m // tm,),
        in_specs=[pl.BlockSpec((tm, k), lambda i: (i, 0)),
                  pl.BlockSpec((k, n), lambda i: (0, 0), pipeline_mode=pl.Buffered(1))],
        out_specs=pl.BlockSpec((tm, n), lambda i: (i, 0)),
        compiler_params=_params(1),
        name="dense",
    )(x, w)


def _softplus(x):
    return jnp.maximum(x, 0.0) + jnp.log1p(jnp.exp(-jnp.abs(x)))


def _gelu_tanh(x):
    return 0.5 * x * (1.0 + jnp.tanh(0.7978845608028654 * (x + 0.044715 * (x * x * x))))


def _rglru_kernel(gate_ref, u_ref, cw_ref, cb_ref, rgw_ref, rgb_ref, igw_ref, igb_ref, lam_ref,
                  o_ref, hist_ref, carry_ref, *, tt):
    t = pl.program_id(1)

    @pl.when(t == 0)
    def _():
        hist_ref[...] = jnp.zeros_like(hist_ref)
        carry_ref[...] = jnp.zeros_like(carry_ref)

    u = u_ref[...]
    ext = jnp.concatenate([hist_ref[...], u], axis=0)
    cw = cw_ref[...]
    xc = cb_ref[...] + u * cw[CONV_WIDTH - 1:CONV_WIDTH, :]
    for d in range(1, CONV_WIDTH):
        shifted = pltpu.roll(ext, d, axis=0)[SUBLANES:, :]
        xc = xc + shifted * cw[CONV_WIDTH - 1 - d:CONV_WIDTH - d, :]
    hist_ref[...] = u[tt - SUBLANES:, :]

    xcb = xc.astype(BF16)
    r = jax.nn.sigmoid(jnp.dot(xcb, rgw_ref[...], preferred_element_type=F32) + rgb_ref[...])
    ig = jax.nn.sigmoid(jnp.dot(xcb, igw_ref[...], preferred_element_type=F32) + igb_ref[...])
    log_a = (-LRU_C) * r * _softplus(-lam_ref[...])
    a = jnp.exp(log_a)
    b = jnp.sqrt(-jnp.tanh(log_a) * (a * a + 1.0)) * (ig * xc)

    groups = tt // SUBLANES
    a = a.reshape(groups, SUBLANES, LANES)
    b = b.reshape(groups, SUBLANES, LANES)
    sub = lax.broadcasted_iota(I32, (groups, SUBLANES, LANES), 1)
    d = 1
    while d < SUBLANES:
        keep = sub >= d
        a_sh = jnp.where(keep, pltpu.roll(a, d, axis=1), 1.0)
        b_sh = jnp.where(keep, pltpu.roll(b, d, axis=1), 0.0)
        b = a * b_sh + b
        a = a * a_sh
        d *= 2
    a = a.reshape(tt, LANES)
    b = b.reshape(tt, LANES)
    carry = carry_ref[...]
    hs = []
    for g in range(tt // SUBLANES):
        rows = slice(g * SUBLANES, (g + 1) * SUBLANES)
        hs.append(a[rows, :] * carry + b[rows, :])
        carry = hs[-1][SUBLANES - 1:SUBLANES, :]
    carry_ref[...] = carry
    o_ref[...] = (jnp.concatenate(hs, axis=0) * _gelu_tanh(gate_ref[...])).astype(o_ref.dtype)


def rglru(proj, conv_w, conv_b, rg_w, rg_b, ig_w, ig_b, lam, *, tt=512):
    t_len = proj.shape[0]
    nblk = rg_w.shape[0]
    width = nblk * LANES
    vec = lambda v: v.reshape(1, width)
    chan = lambda rows: pl.BlockSpec((rows, LANES), lambda n, t: (0, n))
    return pl.pallas_call(
        functools.partial(_rglru_kernel, tt=tt),
        out_shape=jax.ShapeDtypeStruct((t_len, width), BF16),
        grid=(nblk, t_len // tt),
        in_specs=[pl.BlockSpec((tt, LANES), lambda n, t: (t, n)),
                  pl.BlockSpec((tt, LANES), lambda n, t: (t, nblk + n)),
                  chan(CONV_WIDTH), chan(1),
                  pl.BlockSpec((None, LANES, LANES), lambda n, t: (n, 0, 0)), chan(1),
                  pl.BlockSpec((None, LANES, LANES), lambda n, t: (n, 0, 0)), chan(1),
                  chan(1)],
        out_specs=pl.BlockSpec((tt, LANES), lambda n, t: (t, n)),
        scratch_shapes=[pltpu.VMEM((SUBLANES, LANES), F32), pltpu.VMEM((1, LANES), F32)],
        compiler_params=_params(2),
        name="rglru",
    )(proj, proj, conv_w, vec(conv_b), rg_w.astype(BF16), vec(rg_b), ig_w.astype(BF16), vec(ig_b), vec(lam))


def _sb_kernel(q_ref, k_ref, v_ref, o_ref, acc_ref, run_ref, *, tq):
    i = pl.program_id(1)
    scale = HEAD_DIM ** -0.5
    rows = lax.broadcasted_iota(I32, (tq, tq), 0)
    cols = lax.broadcasted_iota(I32, (tq, tq), 1)
    later = (rows > cols).astype(BF16)
    causal = cols < rows
    heads = range(SB_HEADS_PER_STEP)

    def rows_of(j):
        return pl.ds(j * tq if isinstance(j, int) else pl.multiple_of(j * tq, tq), tq)

    def lanes_of(hd):
        return slice(hd * HEAD_DIM, (hd + 1) * HEAD_DIM)

    def front(hd, j, masked):
        z = lax.dot_general(q_ref[:, lanes_of(hd)], k_ref[rows_of(j), lanes_of(hd)], (((1,), (1,)), ((), ())),
                            preferred_element_type=F32) * scale
        sp = jnp.maximum(z, 0.0) + jnp.log(1.0 + jnp.exp(-jnp.abs(z)))
        log_keep = -sp
        log_beta = z - sp
        if masked:
            log_keep = jnp.where(causal, log_keep, 0.0)
        hi = log_keep.astype(BF16)
        lo = (log_keep - hi.astype(F32)).astype(BF16)
        after = (jnp.dot(hi, later, preferred_element_type=F32)
                 + jnp.dot(lo, later, preferred_element_type=F32))
        return log_beta + after, after[:, 0:1] + log_keep[:, 0:1]

    def back(hd, j, logw, run, masked):
        w = jnp.exp(logw if run is None else logw + run)
        if masked:
            w = jnp.where(causal, w, 0.0)
        return jnp.dot(w.astype(BF16), v_ref[rows_of(j), lanes_of(hd)], preferred_element_type=F32)

    @pl.when(i == 0)
    def _():
        for hd in heads:
            logw, _ = front(hd, 0, True)
            o_ref[:, lanes_of(hd)] = back(hd, 0, logw, None, True).astype(o_ref.dtype)

    @pl.when(i > 0)
    def _():
        tops = []
        for hd in heads:
            logw_d, tot_d = front(hd, i, True)
            logw_p, tot_p = front(hd, i - 1, False)
            acc_ref[:, lanes_of(hd)] = back(hd, i, logw_d, None, True) + back(hd, i - 1, logw_p, tot_d, False)
            run0 = tot_d + tot_p
            run_ref[hd] = run0
            tops.append(jnp.max(run0))

        def cond(state):
            j, top = state
            return jnp.logical_and(j >= 0, top > -SB_SKIP)

        def body(state):
            j, _ = state
            tops = []
            for hd in heads:
                logw, tot = front(hd, j, False)
                run = run_ref[hd]
                acc_ref[:, lanes_of(hd)] += back(hd, j, logw, run, False)
                run = run + tot
                run_ref[hd] = run
                tops.append(jnp.max(run))
            return j - 1, functools.reduce(jnp.maximum, tops)

        lax.while_loop(cond, body, (i - 2, functools.reduce(jnp.maximum, tops)))
        o_ref[...] = acc_ref[...].astype(o_ref.dtype)


def stick_breaking(qkv, *, n_heads, q_col, k_col, v_col, tq=256):
    t_len = qkv.shape[0]
    hg = SB_HEADS_PER_STEP
    width = hg * HEAD_DIM
    assert n_heads % hg == 0 and q_col % hg == 0 and k_col % hg == 0 and v_col % hg == 0
    return pl.pallas_call(
        functools.partial(_sb_kernel, tq=tq),
        out_shape=jax.ShapeDtypeStruct((t_len, n_heads * HEAD_DIM), BF16),
        grid=(n_heads // hg, t_len // tq),
        in_specs=[pl.BlockSpec((tq, width), lambda h, i: (i, q_col // hg + h)),
                  pl.BlockSpec((t_len, width), lambda h, i: (0, k_col // hg + h)),
                  pl.BlockSpec((t_len, width), lambda h, i: (0, v_col // hg + h))],
        out_specs=pl.BlockSpec((tq, width), lambda h, i: (i, h)),
        scratch_shapes=[pltpu.VMEM((tq, width), F32), pltpu.VMEM((hg, tq, 1), F32)],
        compiler_params=_params(2),
        name="stick_breaking",
    )(qkv, qkv, qkv)


def _layer_norm(y, g, b):
    mu = jnp.mean(y, axis=-1, keepdims=True)
    yc = y - mu
    var = jnp.mean(yc * yc, axis=-1, keepdims=True)
    return yc * lax.rsqrt(var + LN_EPS) * g + b


def _mixout_kernel(main_ref, qm_ref, mkv_ref, wout_ref, h_ref, g_ref, b_ref, rw2_ref, rb_ref,
                   o_ref, logit_ref, pack_ref, *, alpha):
    main_w = main_ref.shape[1]
    mix = jnp.dot(main_ref[...], wout_ref[0:main_w, :], preferred_element_type=F32)
    q = qm_ref[...].astype(BF16)
    heads = []
    for hd in range(MEM_HEADS):
        lo, hi = hd * HEAD_DIM, (hd + 1) * HEAD_DIM
        kh = mkv_ref[:, lo:hi]
        vh = mkv_ref[:, MEM_WIDTH + lo:MEM_WIDTH + hi]
        s = lax.dot_general(q[:, lo:hi], kh, (((1,), (1,)), ((), ())),
                            preferred_element_type=F32) * (HEAD_DIM ** -0.5)
        e = jnp.exp(s - jnp.max(s, axis=-1, keepdims=True))
        p = e * (1.0 / jnp.sum(e, axis=-1, keepdims=True))
        heads.append(jnp.dot(p.astype(BF16), vh, preferred_element_type=F32).astype(BF16))
    mem_out = jnp.concatenate(heads, axis=-1)
    mix = mix + jnp.dot(mem_out, wout_ref[main_w:, :], preferred_element_type=F32)
    h1 = _layer_norm(alpha * h_ref[...] + mix, g_ref[...], b_ref[...])
    o_ref[...] = h1
    hh = h1.astype(BF16)
    hh32 = hh.astype(F32)
    hl = (h1 - hh32).astype(BF16)
    both = jnp.dot(hh, rw2_ref[...], preferred_element_type=F32)
    logit_ref[...] = (both[:, :LANES] + both[:, LANES:]
                      + jnp.dot(hl, rw2_ref[:, :LANES], preferred_element_type=F32) + rb_ref[...])
    bits = pltpu.bitcast(hh32, jnp.uint32)
    half = bits.shape[1] // 2
    pack_ref[...] = bits[:, :half] | (bits[:, half:] >> 16)


def mixout(main, qsrc, q_col, mem_kv, w_out, h, ln_g, ln_b, router_w, router_b, *, alpha, tm=512):
    t_len, d = h.shape
    main_w = main.shape[1]
    n_e = router_w.shape[1]
    rw = jnp.zeros((d, LANES), F32).at[:, :n_e].set(router_w)
    rwh = rw.astype(BF16)
    rw2 = jnp.concatenate([rwh, (rw - rwh.astype(F32)).astype(BF16)], axis=1)
    rb = jnp.full((1, LANES), -jnp.inf, F32).at[0, :n_e].set(router_b)
    full = lambda a: pl.BlockSpec(a.shape, lambda i: (0, 0))
    args = (main, qsrc, mem_kv, w_out.astype(BF16), h, ln_g.reshape(1, d), ln_b.reshape(1, d), rw2, rb)
    return pl.pallas_call(
        functools.partial(_mixout_kernel, alpha=alpha),
        out_shape=(jax.ShapeDtypeStruct((t_len, d), F32), jax.ShapeDtypeStruct((t_len, LANES), F32),
                   jax.ShapeDtypeStruct((t_len, d // 2), jnp.uint32)),
        grid=(t_len // tm,),
        in_specs=[pl.BlockSpec((tm, main_w), lambda i: (i, 0)),
                  pl.BlockSpec((tm, MEM_WIDTH), lambda i: (i, q_col)),
                  full(mem_kv), full(args[3]),
                  pl.BlockSpec((tm, d), lambda i: (i, 0)),
                  full(args[5]), full(args[6]), full(rw2), full(rb)],
        out_specs=(pl.BlockSpec((tm, d), lambda i: (i, 0)), pl.BlockSpec((tm, LANES), lambda i: (i, 0)),
                   pl.BlockSpec((tm, d // 2), lambda i: (i, 0))),
        compiler_params=_params(1),
        name="mixout",
    )(*args)


def _router_kernel(logit_ref, route_ref, cnt_ref, carry_ref, *, tb):
    @pl.when(pl.program_id(0) == 0)
    def _():
        carry_ref[...] = jnp.zeros_like(carry_ref)

    vals = logit_ref[...]
    lane = lax.broadcasted_iota(I32, (tb, LANES), 1).astype(F32)
    sels, tops, ids = [], [], []
    for _ in range(TOP_K):
        m = jnp.max(vals, axis=-1, keepdims=True)
        idx = jnp.min(jnp.where(vals == m, lane, float(LANES)), axis=-1, keepdims=True)
        sel = lane == idx
        vals = jnp.where(sel, -jnp.inf, vals)
        sels.append(sel)
        tops.append(m)
        ids.append(idx)
    exps = [jnp.exp(m - tops[0]) for m in tops]
    inv = 1.0 / (exps[0] + exps[1] + exps[2] + exps[3])
    onehot = jnp.zeros((tb, LANES), F32)
    for sel in sels:
        onehot = onehot + sel.astype(F32)
    rows = lax.broadcasted_iota(I32, (tb, tb), 0)
    cols = lax.broadcasted_iota(I32, (tb, tb), 1)
    before = (cols < rows).astype(BF16)
    cum = jnp.dot(before, onehot.astype(BF16), preferred_element_type=F32) + carry_ref[...]
    out = jnp.zeros((tb, LANES), F32)
    for k in range(TOP_K):
        pos = jnp.sum(jnp.where(sels[k], cum, 0.0), axis=-1, keepdims=True)
        out = jnp.where(lane == k, ids[k], out)
        out = jnp.where(lane == TOP_K + k, exps[k] * inv, out)
        out = jnp.where(lane == 2 * TOP_K + k, pos, out)
    route_ref[...] = out
    carry_ref[...] += jnp.sum(onehot, axis=0, keepdims=True)
    cnt_ref[...] = jnp.broadcast_to(carry_ref[...], cnt_ref.shape)


def router(logits, *, tb=512):
    t_len = logits.shape[0]
    return pl.pallas_call(
        functools.partial(_router_kernel, tb=tb),
        out_shape=(jax.ShapeDtypeStruct((t_len, LANES), F32), jax.ShapeDtypeStruct((SUBLANES, LANES), F32)),
        grid=(t_len // tb,),
        in_specs=[pl.BlockSpec((tb, LANES), lambda i: (i, 0))],
        out_specs=(pl.BlockSpec((tb, LANES), lambda i: (i, 0)), pl.BlockSpec((SUBLANES, LANES), lambda i: (0, 0))),
        scratch_shapes=[pltpu.VMEM((1, LANES), F32)],
        compiler_params=_params(1),
        name="router",
    )(logits)


def _dispatch_kernel(te_ref, na_ref, dest_ref, h_ref, xs_ref, zero_ref, sem, zsem, *, tb, tm, n_tiles):
    @pl.when(pl.program_id(0) == 0)
    def _():
        zero_ref[...] = jnp.zeros_like(zero_ref)

        def has_padding(j):
            return jnp.logical_or(j >= na_ref[0] - 1, te_ref[j] != te_ref[jnp.minimum(j + 1, n_tiles - 1)])

        def zero_copy(j):
            return pltpu.make_async_copy(zero_ref, xs_ref.at[pl.ds(pl.multiple_of(j * tm, tm), tm), :], zsem)

        def zstart(j, c):
            @pl.when(has_padding(j))
            def _():
                zero_copy(j).start()
            return c

        def zwait(j, c):
            @pl.when(has_padding(j))
            def _():
                zero_copy(j).wait()
            return c

        lax.fori_loop(0, n_tiles, zstart, 0)
        lax.fori_loop(0, n_tiles, zwait, 0)

    def start(r, c):
        for k in range(TOP_K):
            slot = dest_ref[0, 0, r * TOP_K + k]
            pltpu.make_async_copy(h_ref.at[pl.ds(r, 1), :], xs_ref.at[pl.ds(slot, 1), :], sem).start(priority=k % 2)
        return c

    lax.fori_loop(0, tb, start, 0)
    rows = xs_ref.at[pl.ds(0, tb * TOP_K), :]
    pltpu.make_async_copy(rows, rows, sem).wait()


def dispatch(h, dest, tile_e, n_active, n_rows, *, tm, tb=256):
    t_len, d = h.shape
    dest3 = dest.reshape(t_len // tb, 1, tb * TOP_K)
    return pl.pallas_call(
        functools.partial(_dispatch_kernel, tb=tb, tm=tm, n_tiles=n_rows // tm),
        out_shape=jax.ShapeDtypeStruct((n_rows, d), h.dtype),
        grid_spec=pltpu.PrefetchScalarGridSpec(
            num_scalar_prefetch=2,
            grid=(t_len // tb,),
            in_specs=[pl.BlockSpec((1, 1, tb * TOP_K), lambda i, te, na: (i, 0, 0), memory_space=pltpu.SMEM),
                      pl.BlockSpec((tb, d), lambda i, te, na: (i, 0))],
            out_specs=pl.BlockSpec(memory_space=pl.ANY),
            scratch_shapes=[pltpu.VMEM((tm, d), h.dtype), pltpu.SemaphoreType.DMA(()),
                            pltpu.SemaphoreType.DMA(())]),
        compiler_params=_params(1),
        name="dispatch",
    )(tile_e, n_active, dest3, h)


def _tile_idx(i, te_ref, na_ref):
    return jnp.minimum(i, na_ref[0] - 1)


def _zero_inactive(i, na_ref, o_ref):
    @pl.when(i >= na_ref[0])
    def _():
        o_ref[...] = jnp.zeros_like(o_ref)


def _expert_changed(i, te_ref):
    return jnp.logical_or(i == 0, te_ref[i] != te_ref[jnp.maximum(i - 1, 0)])


def _gateup_kernel(te_ref, na_ref, nx_ref, nv_ref, x_ref, w_hbm, b_ref, o_ref, stage_ref, wb_ref, xb_ref, sem):
    i = pl.program_id(0)
    nh = stage_ref.shape[1]
    _zero_inactive(i, na_ref, o_ref)

    def wcopy(e, hf):
        return pltpu.make_async_copy(w_hbm.at[e, :, pl.ds(hf * nh, nh)], stage_ref, sem)

    def reorder(hf):
        rows = lax.broadcasted_iota(I32, (GU, GU), 0)
        cols = lax.broadcasted_iota(I32, (GU, GU), 1)
        src = jnp.where(cols < LANES, 2 * cols, 2 * (cols - LANES) + 1)
        perm = (rows == src).astype(BF16)
        for n in range(0, nh, GU):
            wb_ref[:, hf * nh + n:hf * nh + n + GU] = jnp.dot(
                stage_ref[:, n:n + GU].astype(BF16), perm, preferred_element_type=F32).astype(BF16)

    def columns_of(hf, rows):
        x = xb_ref[0:rows, :]
        for n in range(hf * nh, (hf + 1) * nh, GU):
            gu = jnp.dot(x, wb_ref[:, n:n + GU], preferred_element_type=F32) + b_ref[:, n:n + GU]
            gate = jnp.minimum(gu[:, :LANES], SWIGLU_LIMIT)
            up = jnp.clip(gu[:, LANES:], -SWIGLU_LIMIT, SWIGLU_LIMIT)
            act = (up + 1.0) * gate * jax.nn.sigmoid(SWIGLU_ALPHA * gate)
            o_ref[0:rows, n // 2:n // 2 + LANES] = act.astype(o_ref.dtype)

    def columns(hf):
        tm = o_ref.shape[0]
        upper_used = nv_ref[i] > tm // 2

        @pl.when(upper_used)
        def _():
            columns_of(hf, tm)

        @pl.when(jnp.logical_not(upper_used))
        def _():
            columns_of(hf, tm // 2)
            o_ref[tm // 2:, hf * nh // 2:(hf + 1) * nh // 2] = jnp.zeros((tm - tm // 2, nh // 2), o_ref.dtype)

    @pl.when(i < na_ref[0])
    def _():
        changed = _expert_changed(i, te_ref)
        e = te_ref[i]
        words = x_ref[...]
        upper = pltpu.bitcast(words & jnp.uint32(0xFFFF0000), F32)
        lower = pltpu.bitcast(words << 16, F32)
        xb_ref[...] = jnp.concatenate([upper, lower], axis=1).astype(BF16)

        @pl.when(i == 0)
        def _():
            wcopy(e, 0).start()

        @pl.when(changed)
        def _():
            wcopy(e, 0).wait()
            reorder(0)
            wcopy(e, 1).start()

        columns(0)

        @pl.when(changed)
        def _():
            wcopy(e, 1).wait()
            reorder(1)

            @pl.when(nx_ref[i] >= 0)
            def _():
                wcopy(nx_ref[i], 0).start()

        columns(1)


def moe_gateup(tile_e, n_active, next_e, n_valid, x, w, b, *, tm):
    n_rows = x.shape[0]
    k, n2 = w.shape[1], w.shape[2]
    return pl.pallas_call(
        _gateup_kernel,
        out_shape=jax.ShapeDtypeStruct((n_rows, n2 // 2), BF16),
        grid_spec=pltpu.PrefetchScalarGridSpec(
            num_scalar_prefetch=4,
            grid=(n_rows // tm,),
            in_specs=[pl.BlockSpec((tm, k // 2), lambda i, te, na, nx, nv: (_tile_idx(i, te, na), 0)),
                      pl.BlockSpec(memory_space=pl.ANY),
                      pl.BlockSpec((None, 1, n2), lambda i, te, na, nx, nv: (te[_tile_idx(i, te, na)], 0, 0))],
            out_specs=pl.BlockSpec((tm, n2 // 2), lambda i, te, na, nx, nv: (i, 0)),
            scratch_shapes=[pltpu.VMEM((k, n2 // 2), F32), pltpu.VMEM((k, n2), BF16),
                            pltpu.VMEM((tm, k), BF16), pltpu.SemaphoreType.DMA(())]),
        compiler_params=_params(1),
        name="moe_gateup",
    )(tile_e, n_active, next_e, n_valid, x, w, b)


def _down_kernel(te_ref, na_ref, nx_ref, nv_ref, x_ref, w_hbm, b_ref, o_ref, stage_ref, wb_ref, sem, *, tn):
    i = pl.program_id(0)
    _zero_inactive(i, na_ref, o_ref)

    def wcopy(e):
        return pltpu.make_async_copy(w_hbm.at[e], stage_ref, sem)

    @pl.when(i < na_ref[0])
    def _():
        e = te_ref[i]

        @pl.when(i == 0)
        def _():
            wcopy(e).start()

        @pl.when(_expert_changed(i, te_ref))
        def _():
            wcopy(e).wait()
            wb_ref[...] = stage_ref[...].astype(BF16)

            @pl.when(nx_ref[i] >= 0)
            def _():
                wcopy(nx_ref[i]).start()

        def rows_of(rows):
            x = x_ref[0:rows, :]
            for n in range(0, wb_ref.shape[1], tn):
                o_ref[0:rows, n:n + tn] = (jnp.dot(x, wb_ref[:, n:n + tn], preferred_element_type=F32)
                                           + b_ref[:, n:n + tn])

        tm = o_ref.shape[0]
        upper_used = nv_ref[i] > tm // 2

        @pl.when(upper_used)
        def _():
            rows_of(tm)

        @pl.when(jnp.logical_not(upper_used))
        def _():
            rows_of(tm // 2)
            o_ref[tm // 2:, :] = jnp.zeros((tm - tm // 2, o_ref.shape[1]), o_ref.dtype)


def moe_down(tile_e, n_active, next_e, n_valid, x, w, b, *, tm, tn=256):
    n_rows, k = x.shape
    n_out = w.shape[2]
    return pl.pallas_call(
        functools.partial(_down_kernel, tn=tn),
        out_shape=jax.ShapeDtypeStruct((n_rows, n_out), F32),
        grid_spec=pltpu.PrefetchScalarGridSpec(
            num_scalar_prefetch=4,
            grid=(n_rows // tm,),
            in_specs=[pl.BlockSpec((tm, k), lambda i, te, na, nx, nv: (_tile_idx(i, te, na), 0)),
                      pl.BlockSpec(memory_space=pl.ANY),
                      pl.BlockSpec((None, 1, n_out), lambda i, te, na, nx, nv: (te[_tile_idx(i, te, na)], 0, 0))],
            out_specs=pl.BlockSpec((tm, n_out), lambda i, te, na, nx, nv: (i, 0)),
            scratch_shapes=[pltpu.VMEM((k, n_out), F32), pltpu.VMEM((k, n_out), BF16),
                            pltpu.SemaphoreType.DMA(())]),
        compiler_params=_params(1),
        name="moe_down",
    )(tile_e, n_active, next_e, n_valid, x, w, b)


def _combine_kernel(dest_ref, next_ref, gate_ref, h_ref, g_ref, b_ref, y_ref, o_ref, buf_ref, sems, *, tb, alpha):
    i = pl.program_id(0)

    def gather(idx_ref, s):
        def start(r, c):
            for k in range(TOP_K):
                row = idx_ref[0, 0, r * TOP_K + k]
                pltpu.make_async_copy(y_ref.at[pl.ds(row, 1), :], buf_ref.at[s, k, pl.ds(r, 1), :],
                                      sems.at[s]).start(priority=k % 2)
            return c
        lax.fori_loop(0, tb, start, 0)

    @pl.when(i == 0)
    def _():
        gather(dest_ref, 0)

    for s in range(2):
        @pl.when(i % 2 == s)
        def _():
            @pl.when(i + 1 < pl.num_programs(0))
            def _():
                gather(next_ref, 1 - s)

            pltpu.make_async_copy(buf_ref.at[s], buf_ref.at[s], sems.at[s]).wait()
            gates = gate_ref[...]
            ffn = buf_ref[s, 0] * gates[:, TOP_K:TOP_K + 1]
            for k in range(1, TOP_K):
                ffn = ffn + buf_ref[s, k] * gates[:, TOP_K + k:TOP_K + k + 1]
            o_ref[...] = _layer_norm(alpha * h_ref[...] + ffn, g_ref[...], b_ref[...])


def combine(y, dest, route, h, ln_g, ln_b, *, alpha, tb=256):
    t_len, d = h.shape
    nb = t_len // tb
    dest3 = dest.reshape(nb, 1, tb * TOP_K)
    idx_spec = lambda f: pl.BlockSpec((1, 1, tb * TOP_K), f, memory_space=pltpu.SMEM)
    return pl.pallas_call(
        functools.partial(_combine_kernel, tb=tb, alpha=alpha),
        out_shape=jax.ShapeDtypeStruct((t_len, d), F32),
        grid=(nb,),
        in_specs=[idx_spec(lambda i: (i, 0, 0)),
                  idx_spec(lambda i: (jnp.minimum(i + 1, nb - 1), 0, 0)),
                  pl.BlockSpec((tb, LANES), lambda i: (i, 0)),
                  pl.BlockSpec((tb, d), lambda i: (i, 0)),
                  pl.BlockSpec((1, d), lambda i: (0, 0)),
                  pl.BlockSpec((1, d), lambda i: (0, 0)),
                  pl.BlockSpec(memory_space=pl.ANY)],
        out_specs=pl.BlockSpec((tb, d), lambda i: (i, 0)),
        scratch_shapes=[pltpu.VMEM((2, TOP_K, tb, d), F32), pltpu.SemaphoreType.DMA((2,))],
        compiler_params=_params(1),
        name="combine",
    )(dest3, dest3, route, h, ln_g.reshape(1, d), ln_b.reshape(1, d), y)


def moe(h1, h1_packed, logits, layer, w_gate_up, b_gate_up, w_down, b_down, ln_g, ln_b, *, alpha, tm=512):
    t_len, d = h1.shape
    n_e = w_gate_up.shape[1]
    route, cnt = router(logits)
    eid = route[:, 0:TOP_K].astype(I32)
    pos = route[:, 2 * TOP_K:3 * TOP_K].astype(I32)
    counts = cnt[0, :n_e].astype(I32)
    padded = (counts + tm - 1) // tm * tm
    pend = jnp.cumsum(padded)
    pstart = pend - padded
    onehot = eid[:, :, None] == jnp.arange(n_e, dtype=I32)[None, None, :]
    dest = jnp.sum(jnp.where(onehot, pstart[None, None, :], 0), axis=-1) + pos
    n_tiles = t_len * TOP_K // tm + n_e
    tile_start = jnp.arange(n_tiles, dtype=I32) * tm
    tile_e = jnp.minimum(jnp.sum((pend[None, :] <= tile_start[:, None]).astype(I32), axis=1), n_e - 1)
    n_active = (pend[-1:] // tm).astype(I32)

    xs = dispatch(h1_packed, dest, tile_e, n_active, n_tiles * tm, tm=tm)
    bgu = b_gate_up.reshape(-1, n_e, b_gate_up.shape[-1] // GU, LANES, 2).swapaxes(-1, -2)
    bgu = bgu.reshape(-1, 1, b_gate_up.shape[-1])
    tile_w = tile_e + layer * n_e
    tidx = jnp.arange(n_tiles, dtype=I32)
    follows = ((tile_w[None, :] != tile_w[:, None]) & (tidx[None, :] > tidx[:, None])
               & (tidx[None, :] < n_active[0]))
    next_e = jnp.where(jnp.any(follows, axis=1), tile_w[jnp.argmax(follows, axis=1)], -1).astype(I32)
    group_end = (pstart + counts)[tile_e]
    n_valid = jnp.clip(group_end - tile_start, 0, tm).astype(I32)
    act = moe_gateup(tile_w, n_active, next_e, n_valid, xs, w_gate_up.reshape((-1,) + w_gate_up.shape[2:]),
                     bgu, tm=tm)
    y = moe_down(tile_w, n_active, next_e, n_valid, act, w_down.reshape((-1,) + w_down.shape[2:]),
                 b_down.reshape(-1, 1, b_down.shape[-1]), tm=tm)
    return combine(y, dest, route, h1, ln_g, ln_b, alpha=alpha)


def kernel(x, mem, a_w_in, a_conv_w, a_conv_b, a_rg_w, a_rg_b, a_ig_w, a_ig_b, a_lambda, a_w_out, b_w_q, b_w_out, w_kv_shared, mem_w_kv, ln1_g, ln1_b, ln2_g, ln2_b, router_w, router_b, w_gate_up, b_gate_up, w_down, b_down):
    bsz, seq, d = x.shape
    depth = ln1_g.shape[0]
    n_a = a_w_in.shape[0]
    alpha = float((2 * depth) ** 0.25)
    outs = []
    for bi in range(bsz):
        h = x[bi]
        mem_b = mem[bi]
        qkv = None
        for layer in range(depth):
            mem_kv = dense(mem_b, mem_w_kv[layer].astype(BF16), BF16, tm=mem_b.shape[0])
            if layer < n_a:
                lw = a_rg_w.shape[1] * LANES
                proj = dense(h, a_w_in[layer].astype(BF16), F32, tm=512)
                main = rglru(proj, a_conv_w[layer], a_conv_b[layer], a_rg_w[layer], a_rg_b[layer],
                             a_ig_w[layer], a_ig_b[layer], a_lambda[layer])
                qsrc, q_col, w_out = proj, 2 * lw // MEM_WIDTH, a_w_out[layer]
            else:
                j = layer - n_a
                sbw = w_kv_shared.shape[1] // 2
                n_heads = sbw // HEAD_DIM
                if j == 0:
                    w_cat = jnp.concatenate([w_kv_shared, b_w_q[j]], axis=1).astype(BF16)
                    qkv = dense(h, w_cat, BF16, tm=512)
                else:
                    q_only = dense(h, b_w_q[j].astype(BF16), BF16, tm=512)
                    qkv = jnp.concatenate([qkv[:, :2 * sbw], q_only], axis=1)
                main = stick_breaking(qkv, n_heads=n_heads, q_col=2 * n_heads, k_col=0, v_col=n_heads)
                qsrc, q_col, w_out = qkv, (3 * sbw) // MEM_WIDTH, b_w_out[j]
            h1, logits, h1_packed = mixout(main, qsrc, q_col, mem_kv, w_out, h, ln1_g[layer], ln1_b[layer],
                                           router_w[layer], router_b[layer], alpha=alpha)
            h = moe(h1, h1_packed, logits, layer, w_gate_up, b_gate_up, w_down, b_down,
                    ln2_g[layer], ln2_b[layer], alpha=alpha)
        outs.append(h)
    return jnp.stack(outs, axis=0)
```

```python
import functools

import jax
import jax.numpy as jnp
from jax import lax
from jax.experimental import pallas as pl
from jax.experimental.pallas import tpu as pltpu

F32 = jnp.float32
BF16 = jnp.bfloat16
I32 = jnp.int32

HEAD_DIM = 128
MEM_HEADS = 4
MEM_WIDTH = MEM_HEADS * HEAD_DIM
CONV_WIDTH = 4
LRU_C = 8.0
N_EXPERTS = 32
TOP_K = 4
SWIGLU_LIMIT = 7.0
SWIGLU_ALPHA = 1.702
LN_EPS = 1e-5
LANES = 128
SUBLANES = 8
GU = 2 * LANES
VMEM_LIMIT = 56 * 1024 * 1024
SB_SKIP = 100.0
SB_HEADS_PER_STEP = 2


def _params(n_axes, vmem=VMEM_LIMIT):
    return pltpu.CompilerParams(dimension_semantics=("arbitrary",) * n_axes, vmem_limit_bytes=vmem)


def _dense_kernel(x_ref, w_ref, o_ref, *, tn):
    xb = x_ref[...].astype(BF16)
    for n in range(0, w_ref.shape[1], tn):
        o_ref[:, n:n + tn] = jnp.dot(xb, w_ref[:, n:n + tn], preferred_element_type=F32).astype(o_ref.dtype)


def dense(x, w, out_dtype, *, tm, tn=256):
    m, k = x.shape
    n = w.shape[1]
    return pl.pallas_call(
        functools.partial(_dense_kernel, tn=tn),
        out_shape=jax.ShapeDtypeStruct((m, n), out_dtype),
        grid=(m // tm,),
        in_specs=[pl.BlockSpec((tm, k), lambda i: (i, 0)),
                  pl.BlockSpec((k, n), lambda i: (0, 0), pipeline_mode=pl.Buffered(1))],
        out_specs=pl.BlockSpec((tm, n), lambda i: (i, 0)),
        compiler_params=_params(1),
        name="dense",
    )(x, w)


def _softplus(x):
    return jnp.maximum(x, 0.0) + jnp.log1p(jnp.exp(-jnp.abs(x)))


def _gelu_tanh(x):
    return 0.5 * x * (1.0 + jnp.tanh(0.7978845608028654 * (x + 0.044715 * (x * x * x))))


def _rglru_kernel(gate_ref, u_ref, cw_ref, cb_ref, rgw_ref, rgb_ref, igw_ref, igb_ref, lam_ref,
                  o_ref, hist_ref, carry_ref, *, tt):
    t = pl.program_id(1)

    @pl.when(t == 0)
    def _():
        hist_ref[...] = jnp.zeros_like(hist_ref)
        carry_ref[...] = jnp.zeros_like(carry_ref)

    u = u_ref[...]
    ext = jnp.concatenate([hist_ref[...], u], axis=0)
    cw = cw_ref[...]
    xc = cb_ref[...] + u * cw[CONV_WIDTH - 1:CONV_WIDTH, :]
    for d in range(1, CONV_WIDTH):
        shifted = pltpu.roll(ext, d, axis=0)[SUBLANES:, :]
        xc = xc + shifted * cw[CONV_WIDTH - 1 - d:CONV_WIDTH - d, :]
    hist_ref[...] = u[tt - SUBLANES:, :]

    xcb = xc.astype(BF16)
    r = jax.nn.sigmoid(jnp.dot(xcb, rgw_ref[...], preferred_element_type=F32) + rgb_ref[...])
    ig = jax.nn.sigmoid(jnp.dot(xcb, igw_ref[...], preferred_element_type=F32) + igb_ref[...])
    log_a = (-LRU_C) * r * _softplus(-lam_ref[...])
    a = jnp.exp(log_a)
    b = jnp.sqrt(-jnp.tanh(log_a) * (a * a + 1.0)) * (ig * xc)

    groups = tt // SUBLANES
    a = a.reshape(groups, SUBLANES, LANES)
    b = b.reshape(groups, SUBLANES, LANES)
    sub = lax.broadcasted_iota(I32, (groups, SUBLANES, LANES), 1)
    d = 1
    while d < SUBLANES:
        keep = sub >= d
        a_sh = jnp.where(keep, pltpu.roll(a, d, axis=1), 1.0)
        b_sh = jnp.where(keep, pltpu.roll(b, d, axis=1), 0.0)
        b = a * b_sh + b
        a = a * a_sh
        d *= 2
    a = a.reshape(tt, LANES)
    b = b.reshape(tt, LANES)
    carry = carry_ref[...]
    hs = []
    for g in range(tt // SUBLANES):
        rows = slice(g * SUBLANES, (g + 1) * SUBLANES)
        hs.append(a[rows, :] * carry + b[rows, :])
        carry = hs[-1][SUBLANES - 1:SUBLANES, :]
    carry_ref[...] = carry
    o_ref[...] = (jnp.concatenate(hs, axis=0) * _gelu_tanh(gate_ref[...])).astype(o_ref.dtype)


def rglru(proj, conv_w, conv_b, rg_w, rg_b, ig_w, ig_b, lam, *, tt=512):
    t_len = proj.shape[0]
    nblk = rg_w.shape[0]
    width = nblk * LANES
    vec = lambda v: v.reshape(1, width)
    chan = lambda rows: pl.BlockSpec((rows, LANES), lambda n, t: (0, n))
    return pl.pallas_call(
        functools.partial(_rglru_kernel, tt=tt),
        out_shape=jax.ShapeDtypeStruct((t_len, width), BF16),
        grid=(nblk, t_len // tt),
        in_specs=[pl.BlockSpec((tt, LANES), lambda n, t: (t, n)),
                  pl.BlockSpec((tt, LANES), lambda n, t: (t, nblk + n)),
                  chan(CONV_WIDTH), chan(1),
                  pl.BlockSpec((None, LANES, LANES), lambda n, t: (n, 0, 0)), chan(1),
                  pl.BlockSpec((None, LANES, LANES), lambda n, t: (n, 0, 0)), chan(1),
                  chan(1)],
        out_specs=pl.BlockSpec((tt, LANES), lambda n, t: (t, n)),
        scratch_shapes=[pltpu.VMEM((SUBLANES, LANES), F32), pltpu.VMEM((1, LANES), F32)],
        compiler_params=_params(2),
        name="rglru",
    )(proj, proj, conv_w, vec(conv_b), rg_w.astype(BF16), vec(rg_b), ig_w.astype(BF16), vec(ig_b), vec(lam))


def _sb_kernel(q_ref, k_ref, v_ref, o_ref, acc_ref, run_ref, *, tq):
    i = pl.program_id(1)
    scale = HEAD_DIM ** -0.5
    rows = lax.broadcasted_iota(I32, (tq, tq), 0)
    cols = lax.broadcasted_iota(I32, (tq, tq), 1)
    later = (rows > cols).astype(BF16)
    causal = cols < rows
    heads = range(SB_HEADS_PER_STEP)

    def rows_of(j):
        return pl.ds(j * tq if isinstance(j, int) else pl.multiple_of(j * tq, tq), tq)

    def lanes_of(hd):
        return slice(hd * HEAD_DIM, (hd + 1) * HEAD_DIM)

    def front(hd, j, masked):
        z = lax.dot_general(q_ref[:, lanes_of(hd)], k_ref[rows_of(j), lanes_of(hd)], (((1,), (1,)), ((), ())),
                            preferred_element_type=F32) * scale
        sp = jnp.maximum(z, 0.0) + jnp.log(1.0 + jnp.exp(-jnp.abs(z)))
        log_keep = -sp
        log_beta = z - sp
        if masked:
            log_keep = jnp.where(causal, log_keep, 0.0)
        hi = log_keep.astype(BF16)
        lo = (log_keep - hi.astype(F32)).astype(BF16)
        after = (jnp.dot(hi, later, preferred_element_type=F32)
                 + jnp.dot(lo, later, preferred_element_type=F32))
        return log_beta + after, after[:, 0:1] + log_keep[:, 0:1]

    def back(hd, j, logw, run, masked):
        w = jnp.exp(logw if run is None else logw + run)
        if masked:
            w = jnp.where(causal, w, 0.0)
        return jnp.dot(w.astype(BF16), v_ref[rows_of(j), lanes_of(hd)], preferred_element_type=F32)

    @pl.when(i == 0)
    def _():
        for hd in heads:
            logw, _ = front(hd, 0, True)
            o_ref[:, lanes_of(hd)] = back(hd, 0, logw, None, True).astype(o_ref.dtype)

    @pl.when(i > 0)
    def _():
        tops = []
        for hd in heads:
            logw_d, tot_d = front(hd, i, True)
            logw_p, tot_p = front(hd, i - 1, False)
            acc_ref[:, lanes_of(hd)] = back(hd, i, logw_d, None, True) + back(hd, i - 1, logw_p, tot_d, False)
            run0 = tot_d + tot_p
            run_ref[hd] = run0
            tops.append(jnp.max(run0))

        def cond(state):
            j, top = state
            return jnp.logical_and(j >= 0, top > -SB_SKIP)

        def body(state):
            j, _ = state
            tops = []
            for hd in heads:
                logw, tot = front(hd, j, False)
                run = run_ref[hd]
                acc_ref[:, lanes_of(hd)] += back(hd, j, logw, run, False)
                run = run + tot
                run_ref[hd] = run
                tops.append(jnp.max(run))
            return j - 1, functools.reduce(jnp.maximum, tops)

        lax.while_loop(cond, body, (i - 2, functools.reduce(jnp.maximum, tops)))
        o_ref[...] = acc_ref[...].astype(o_ref.dtype)


def stick_breaking(qkv, *, n_heads, q_col, k_col, v_col, tq=256):
    t_len = qkv.shape[0]
    hg = SB_HEADS_PER_STEP
    width = hg * HEAD_DIM
    assert n_heads % hg == 0 and q_col % hg == 0 and k_col % hg == 0 and v_col % hg == 0
    return pl.pallas_call(
        functools.partial(_sb_kernel, tq=tq),
        out_shape=jax.ShapeDtypeStruct((t_len, n_heads * HEAD_DIM), BF16),
        grid=(n_heads // hg, t_len // tq),
        in_specs=[pl.BlockSpec((tq, width), lambda h, i: (i, q_col // hg + h)),
                  pl.BlockSpec((t_len, width), lambda h, i: (0, k_col // hg + h)),
                  pl.BlockSpec((t_len, width), lambda h, i: (0, v_col // hg + h))],
        out_specs=pl.BlockSpec((tq, width), lambda h, i: (i, h)),
        scratch_shapes=[pltpu.VMEM((tq, width), F32), pltpu.VMEM((hg, tq, 1), F32)],
        compiler_params=_params(2),
        name="stick_breaking",
    )(qkv, qkv, qkv)


def _layer_norm(y, g, b):
    mu = jnp.mean(y, axis=-1, keepdims=True)
    yc = y - mu
    var = jnp.mean(yc * yc, axis=-1, keepdims=True)
    return yc * lax.rsqrt(var + LN_EPS) * g + b


def _mixout_kernel(main_ref, qm_ref, mkv_ref, wout_ref, h_ref, g_ref, b_ref, rw2_ref, rb_ref,
                   o_ref, logit_ref, pack_ref, *, alpha):
    main_w = main_ref.shape[1]
    mix = jnp.dot(main_ref[...], wout_ref[0:main_w, :], preferred_element_type=F32)
    q = qm_ref[...].astype(BF16)
    heads = []
    for hd in range(MEM_HEADS):
        lo, hi = hd * HEAD_DIM, (hd + 1) * HEAD_DIM
        kh = mkv_ref[:, lo:hi]
        vh = mkv_ref[:, MEM_WIDTH + lo:MEM_WIDTH + hi]
        s = lax.dot_general(q[:, lo:hi], kh, (((1,), (1,)), ((), ())),
                            preferred_element_type=F32) * (HEAD_DIM ** -0.5)
        e = jnp.exp(s - jnp.max(s, axis=-1, keepdims=True))
        p = e * (1.0 / jnp.sum(e, axis=-1, keepdims=True))
        heads.append(jnp.dot(p.astype(BF16), vh, preferred_element_type=F32).astype(BF16))
    mem_out = jnp.concatenate(heads, axis=-1)
    mix = mix + jnp.dot(mem_out, wout_ref[main_w:, :], preferred_element_type=F32)
    h1 = _layer_norm(alpha * h_ref[...] + mix, g_ref[...], b_ref[...])
    o_ref[...] = h1
    hh = h1.astype(BF16)
    hh32 = hh.astype(F32)
    hl = (h1 - hh32).astype(BF16)
    both = jnp.dot(hh, rw2_ref[...], preferred_element_type=F32)
    logit_ref[...] = (both[:, :LANES] + both[:, LANES:]
                      + jnp.dot(hl, rw2_ref[:, :LANES], preferred_element_type=F32) + rb_ref[...])
    bits = pltpu.bitcast(hh32, jnp.uint32)
    half = bits.shape[1] // 2
    packed = bits[:, :half] | (bits[:, half:] >> 16)
    pack_ref[...] = packed.reshape(packed.shape[0], half // LANES, LANES)


def mixout(main, qsrc, q_col, mem_kv, w_out, h, ln_g, ln_b, router_w, router_b, *, alpha, tm=512):
    t_len, d = h.shape
    main_w = main.shape[1]
    n_e = router_w.shape[1]
    rw = jnp.zeros((d, LANES), F32).at[:, :n_e].set(router_w)
    rwh = rw.astype(BF16)
    rw2 = jnp.concatenate([rwh, (rw - rwh.astype(F32)).astype(BF16)], axis=1)
    rb = jnp.full((1, LANES), -jnp.inf, F32).at[0, :n_e].set(router_b)
    full = lambda a: pl.BlockSpec(a.shape, lambda i: (0, 0))
    args = (main, qsrc, mem_kv, w_out.astype(BF16), h, ln_g.reshape(1, d), ln_b.reshape(1, d), rw2, rb)
    return pl.pallas_call(
        functools.partial(_mixout_kernel, alpha=alpha),
        out_shape=(jax.ShapeDtypeStruct((t_len, d), F32), jax.ShapeDtypeStruct((t_len, LANES), F32),
                   jax.ShapeDtypeStruct((t_len, d // 2 // LANES, LANES), jnp.uint32)),
        grid=(t_len // tm,),
        in_specs=[pl.BlockSpec((tm, main_w), lambda i: (i, 0)),
                  pl.BlockSpec((tm, MEM_WIDTH), lambda i: (i, q_col)),
                  full(mem_kv), full(args[3]),
                  pl.BlockSpec((tm, d), lambda i: (i, 0)),
                  full(args[5]), full(args[6]), full(rw2), full(rb)],
        out_specs=(pl.BlockSpec((tm, d), lambda i: (i, 0)), pl.BlockSpec((tm, LANES), lambda i: (i, 0)),
                   pl.BlockSpec((tm, d // 2 // LANES, LANES), lambda i: (i, 0, 0))),
        compiler_params=_params(1),
        name="mixout",
    )(*args)


def _router_kernel(logit_ref, route_ref, cnt_ref, carry_ref, *, tb):
    @pl.when(pl.program_id(0) == 0)
    def _():
        carry_ref[...] = jnp.zeros_like(carry_ref)

    vals = logit_ref[...]
    lane = lax.broadcasted_iota(I32, (tb, LANES), 1).astype(F32)
    sels, tops, ids = [], [], []
    for _ in range(TOP_K):
        m = jnp.max(vals, axis=-1, keepdims=True)
        idx = jnp.min(jnp.where(vals == m, lane, float(LANES)), axis=-1, keepdims=True)
        sel = lane == idx
        vals = jnp.where(sel, -jnp.inf, vals)
        sels.append(sel)
        tops.append(m)
        ids.append(idx)
    exps = [jnp.exp(m - tops[0]) for m in tops]
    inv = 1.0 / (exps[0] + exps[1] + exps[2] + exps[3])
    onehot = jnp.zeros((tb, LANES), F32)
    for sel in sels:
        onehot = onehot + sel.astype(F32)
    rows = lax.broadcasted_iota(I32, (tb, tb), 0)
    cols = lax.broadcasted_iota(I32, (tb, tb), 1)
    before = (cols < rows).astype(BF16)
    cum = jnp.dot(before, onehot.astype(BF16), preferred_element_type=F32) + carry_ref[...]
    out = jnp.zeros((tb, LANES), F32)
    for k in range(TOP_K):
        pos = jnp.sum(jnp.where(sels[k], cum, 0.0), axis=-1, keepdims=True)
        out = jnp.where(lane == k, ids[k], out)
        out = jnp.where(lane == TOP_K + k, exps[k] * inv, out)
        out = jnp.where(lane == 2 * TOP_K + k, pos, out)
    route_ref[...] = out
    carry_ref[...] += jnp.sum(onehot, axis=0, keepdims=True)
    cnt_ref[...] = jnp.broadcast_to(carry_ref[...], cnt_ref.shape)


def router(logits, *, tb=512):
    t_len = logits.shape[0]
    return pl.pallas_call(
        functools.partial(_router_kernel, tb=tb),
        out_shape=(jax.ShapeDtypeStruct((t_len, LANES), F32), jax.ShapeDtypeStruct((SUBLANES, LANES), F32)),
        grid=(t_len // tb,),
        in_specs=[pl.BlockSpec((tb, LANES), lambda i: (i, 0))],
        out_specs=(pl.BlockSpec((tb, LANES), lambda i: (i, 0)), pl.BlockSpec((SUBLANES, LANES), lambda i: (0, 0))),
        scratch_shapes=[pltpu.VMEM((1, LANES), F32)],
        compiler_params=_params(1),
        name="router",
    )(logits)


def _dispatch_kernel(te_ref, na_ref, dest_ref, h_ref, xs_ref, zero_ref, sem, zsem, *, tb, tm, n_tiles):
    @pl.when(pl.program_id(0) == 0)
    def _():
        zero_ref[...] = jnp.zeros_like(zero_ref)

        def has_padding(j):
            return jnp.logical_or(j >= na_ref[0] - 1, te_ref[j] != te_ref[jnp.minimum(j + 1, n_tiles - 1)])

        def zero_copy(j):
            return pltpu.make_async_copy(zero_ref, xs_ref.at[pl.ds(pl.multiple_of(j * tm, tm), tm)], zsem)

        def zstart(j, c):
            @pl.when(has_padding(j))
            def _():
                zero_copy(j).start()
            return c

        def zwait(j, c):
            @pl.when(has_padding(j))
            def _():
                zero_copy(j).wait()
            return c

        lax.fori_loop(0, n_tiles, zstart, 0)
        lax.fori_loop(0, n_tiles, zwait, 0)

    def start(r, c):
        for k in range(TOP_K):
            slot = dest_ref[0, 0, r * TOP_K + k]
            pltpu.make_async_copy(h_ref.at[pl.ds(r, 1)], xs_ref.at[pl.ds(slot, 1)], sem).start(priority=k % 2)
        return c

    lax.fori_loop(0, tb, start, 0)
    rows = xs_ref.at[pl.ds(0, tb * TOP_K)]
    pltpu.make_async_copy(rows, rows, sem).wait()


def dispatch(h, dest, tile_e, n_active, n_rows, *, tm, tb=256):
    t_len, slab = h.shape[0], h.shape[1:]
    dest3 = dest.reshape(t_len // tb, 1, tb * TOP_K)
    return pl.pallas_call(
        functools.partial(_dispatch_kernel, tb=tb, tm=tm, n_tiles=n_rows // tm),
        out_shape=jax.ShapeDtypeStruct((n_rows,) + slab, h.dtype),
        grid_spec=pltpu.PrefetchScalarGridSpec(
            num_scalar_prefetch=2,
            grid=(t_len // tb,),
            in_specs=[pl.BlockSpec((1, 1, tb * TOP_K), lambda i, te, na: (i, 0, 0), memory_space=pltpu.SMEM),
                      pl.BlockSpec((tb,) + slab, lambda i, te, na: (i, 0, 0))],
            out_specs=pl.BlockSpec(memory_space=pl.ANY),
            scratch_shapes=[pltpu.VMEM((tm,) + slab, h.dtype), pltpu.SemaphoreType.DMA(()),
                            pltpu.SemaphoreType.DMA(())]),
        compiler_params=_params(1),
        name="dispatch",
    )(tile_e, n_active, dest3, h)


def _tile_idx(i, te_ref, na_ref):
    return jnp.minimum(i, na_ref[0] - 1)


def _zero_inactive(i, na_ref, o_ref):
    @pl.when(i >= na_ref[0])
    def _():
        o_ref[...] = jnp.zeros_like(o_ref)


def _expert_changed(i, te_ref):
    return jnp.logical_or(i == 0, te_ref[i] != te_ref[jnp.maximum(i - 1, 0)])


def _gateup_kernel(te_ref, na_ref, nx_ref, nv_ref, x_ref, w_hbm, b_ref, o_ref, stage_ref, wb_ref, xb_ref, sem):
    i = pl.program_id(0)
    nh = stage_ref.shape[1]
    _zero_inactive(i, na_ref, o_ref)

    def wcopy(e, hf):
        return pltpu.make_async_copy(w_hbm.at[e, :, pl.ds(hf * nh, nh)], stage_ref, sem)

    def reorder(hf):
        rows = lax.broadcasted_iota(I32, (GU, GU), 0)
        cols = lax.broadcasted_iota(I32, (GU, GU), 1)
        src = jnp.where(cols < LANES, 2 * cols, 2 * (cols - LANES) + 1)
        perm = (rows == src).astype(BF16)
        for n in range(0, nh, GU):
            wb_ref[:, hf * nh + n:hf * nh + n + GU] = jnp.dot(
                stage_ref[:, n:n + GU].astype(BF16), perm, preferred_element_type=F32).astype(BF16)

    def columns_of(hf, rows):
        x = xb_ref[0:rows, :]
        for n in range(hf * nh, (hf + 1) * nh, GU):
            gu = jnp.dot(x, wb_ref[:, n:n + GU], preferred_element_type=F32) + b_ref[:, n:n + GU]
            gate = jnp.minimum(gu[:, :LANES], SWIGLU_LIMIT)
            up = jnp.clip(gu[:, LANES:], -SWIGLU_LIMIT, SWIGLU_LIMIT)
            act = (up + 1.0) * gate * jax.nn.sigmoid(SWIGLU_ALPHA * gate)
            o_ref[0:rows, n // 2:n // 2 + LANES] = act.astype(o_ref.dtype)

    def columns(hf):
        tm = o_ref.shape[0]
        upper_used = nv_ref[i] > tm // 2

        @pl.when(upper_used)
        def _():
            columns_of(hf, tm)

        @pl.when(jnp.logical_not(upper_used))
        def _():
            columns_of(hf, tm // 2)
            o_ref[tm // 2:, hf * nh // 2:(hf + 1) * nh // 2] = jnp.zeros((tm - tm // 2, nh // 2), o_ref.dtype)

    @pl.when(i < na_ref[0])
    def _():
        changed = _expert_changed(i, te_ref)
        e = te_ref[i]
        words = x_ref[...].reshape(x_ref.shape[0], -1)
        upper = pltpu.bitcast(words & jnp.uint32(0xFFFF0000), F32)
        lower = pltpu.bitcast(words << 16, F32)
        xb_ref[...] = jnp.concatenate([upper, lower], axis=1).astype(BF16)

        @pl.when(i == 0)
        def _():
            wcopy(e, 0).start()

        @pl.when(changed)
        def _():
            wcopy(e, 0).wait()
            reorder(0)
            wcopy(e, 1).start()

        columns(0)

        @pl.when(changed)
        def _():
            wcopy(e, 1).wait()
            reorder(1)

            @pl.when(nx_ref[i] >= 0)
            def _():
                wcopy(nx_ref[i], 0).start()

        columns(1)


def moe_gateup(tile_e, n_active, next_e, n_valid, x, w, b, *, tm):
    n_rows = x.shape[0]
    k, n2 = w.shape[1], w.shape[2]
    return pl.pallas_call(
        _gateup_kernel,
        out_shape=jax.ShapeDtypeStruct((n_rows, n2 // 2), BF16),
        grid_spec=pltpu.PrefetchScalarGridSpec(
            num_scalar_prefetch=4,
            grid=(n_rows // tm,),
            in_specs=[pl.BlockSpec((tm,) + x.shape[1:], lambda i, te, na, nx, nv: (_tile_idx(i, te, na), 0, 0)),
                      pl.BlockSpec(memory_space=pl.ANY),
                      pl.BlockSpec((None, 1, n2), lambda i, te, na, nx, nv: (te[_tile_idx(i, te, na)], 0, 0))],
            out_specs=pl.BlockSpec((tm, n2 // 2), lambda i, te, na, nx, nv: (i, 0)),
            scratch_shapes=[pltpu.VMEM((k, n2 // 2), F32), pltpu.VMEM((k, n2), BF16),
                            pltpu.VMEM((tm, k), BF16), pltpu.SemaphoreType.DMA(())]),
        compiler_params=_params(1),
        name="moe_gateup",
    )(tile_e, n_active, next_e, n_valid, x, w, b)


def _down_kernel(te_ref, na_ref, nx_ref, nv_ref, x_ref, w_hbm, b_ref, o_ref, stage_ref, wb_ref, sem, *, tn):
    i = pl.program_id(0)
    _zero_inactive(i, na_ref, o_ref)

    def wcopy(e):
        return pltpu.make_async_copy(w_hbm.at[e], stage_ref, sem)

    @pl.when(i < na_ref[0])
    def _():
        e = te_ref[i]

        @pl.when(i == 0)
        def _():
            wcopy(e).start()

        @pl.when(_expert_changed(i, te_ref))
        def _():
            wcopy(e).wait()
            wb_ref[...] = stage_ref[...].astype(BF16)

            @pl.when(nx_ref[i] >= 0)
            def _():
                wcopy(nx_ref[i]).start()

        def rows_of(rows):
            x = x_ref[0:rows, :]
            for n in range(0, wb_ref.shape[1], tn):
                y = jnp.dot(x, wb_ref[:, n:n + tn], preferred_element_type=F32) + b_ref[:, n:n + tn]
                o_ref[0:rows, n // LANES:(n + tn) // LANES, :] = y.reshape(rows, tn // LANES, LANES)

        tm = o_ref.shape[0]
        upper_used = nv_ref[i] > tm // 2

        @pl.when(upper_used)
        def _():
            rows_of(tm)

        @pl.when(jnp.logical_not(upper_used))
        def _():
            rows_of(tm // 2)
            o_ref[tm // 2:] = jnp.zeros((tm - tm // 2,) + o_ref.shape[1:], o_ref.dtype)


def moe_down(tile_e, n_active, next_e, n_valid, x, w, b, *, tm, tn=SUBLANES * LANES):
    n_rows, k = x.shape
    n_out = w.shape[2]
    return pl.pallas_call(
        functools.partial(_down_kernel, tn=tn),
        out_shape=jax.ShapeDtypeStruct((n_rows, n_out // LANES, LANES), F32),
        grid_spec=pltpu.PrefetchScalarGridSpec(
            num_scalar_prefetch=4,
            grid=(n_rows // tm,),
            in_specs=[pl.BlockSpec((tm, k), lambda i, te, na, nx, nv: (_tile_idx(i, te, na), 0)),
                      pl.BlockSpec(memory_space=pl.ANY),
                      pl.BlockSpec((None, 1, n_out), lambda i, te, na, nx, nv: (te[_tile_idx(i, te, na)], 0, 0))],
            out_specs=pl.BlockSpec((tm, n_out // LANES, LANES), lambda i, te, na, nx, nv: (i, 0, 0)),
            scratch_shapes=[pltpu.VMEM((k, n_out), F32), pltpu.VMEM((k, n_out), BF16),
                            pltpu.SemaphoreType.DMA(())]),
        compiler_params=_params(1),
        name="moe_down",
    )(tile_e, n_active, next_e, n_valid, x, w, b)


def _combine_kernel(dest_ref, next_ref, gate_ref, h_ref, g_ref, b_ref, y_ref, o_ref, buf_ref, sems, *, tb, alpha):
    i = pl.program_id(0)

    def gather(idx_ref, s):
        def start(r, c):
            for k in range(TOP_K):
                row = idx_ref[0, 0, r * TOP_K + k]
                pltpu.make_async_copy(y_ref.at[pl.ds(row, 1)], buf_ref.at[s, k, pl.ds(r, 1)],
                                      sems.at[s]).start(priority=k % 2)
            return c
        lax.fori_loop(0, tb, start, 0)

    @pl.when(i == 0)
    def _():
        gather(dest_ref, 0)

    for s in range(2):
        @pl.when(i % 2 == s)
        def _():
            @pl.when(i + 1 < pl.num_programs(0))
            def _():
                gather(next_ref, 1 - s)

            pltpu.make_async_copy(buf_ref.at[s], buf_ref.at[s], sems.at[s]).wait()
            gates = gate_ref[...]
            d = h_ref.shape[1]
            ffn = buf_ref[s, 0].reshape(tb, d) * gates[:, TOP_K:TOP_K + 1]
            for k in range(1, TOP_K):
                ffn = ffn + buf_ref[s, k].reshape(tb, d) * gates[:, TOP_K + k:TOP_K + k + 1]
            o_ref[...] = _layer_norm(alpha * h_ref[...] + ffn, g_ref[...], b_ref[...])


def combine(y, dest, route, h, ln_g, ln_b, *, alpha, tb=256):
    t_len, d = h.shape
    nb = t_len // tb
    dest3 = dest.reshape(nb, 1, tb * TOP_K)
    idx_spec = lambda f: pl.BlockSpec((1, 1, tb * TOP_K), f, memory_space=pltpu.SMEM)
    return pl.pallas_call(
        functools.partial(_combine_kernel, tb=tb, alpha=alpha),
        out_shape=jax.ShapeDtypeStruct((t_len, d), F32),
        grid=(nb,),
        in_specs=[idx_spec(lambda i: (i, 0, 0)),
                  idx_spec(lambda i: (jnp.minimum(i + 1, nb - 1), 0, 0)),
                  pl.BlockSpec((tb, LANES), lambda i: (i, 0)),
                  pl.BlockSpec((tb, d), lambda i: (i, 0)),
                  pl.BlockSpec((1, d), lambda i: (0, 0)),
                  pl.BlockSpec((1, d), lambda i: (0, 0)),
                  pl.BlockSpec(memory_space=pl.ANY)],
        out_specs=pl.BlockSpec((tb, d), lambda i: (i, 0)),
        scratch_shapes=[pltpu.VMEM((2, TOP_K, tb) + y.shape[1:], F32), pltpu.SemaphoreType.DMA((2,))],
        compiler_params=_params(1),
        name="combine",
    )(dest3, dest3, route, h, ln_g.reshape(1, d), ln_b.reshape(1, d), y)


def moe(h1, h1_packed, logits, layer, w_gate_up, b_gate_up, w_down, b_down, ln_g, ln_b, *, alpha, tm=512):
    t_len, d = h1.shape
    n_e = w_gate_up.shape[1]
    route, cnt = router(logits)
    eid = route[:, 0:TOP_K].astype(I32)
    pos = route[:, 2 * TOP_K:3 * TOP_K].astype(I32)
    counts = cnt[0, :n_e].astype(I32)
    padded = (counts + tm - 1) // tm * tm
    pend = jnp.cumsum(padded)
    pstart = pend - padded
    onehot = eid[:, :, None] == jnp.arange(n_e, dtype=I32)[None, None, :]
    dest = jnp.sum(jnp.where(onehot, pstart[None, None, :], 0), axis=-1) + pos
    n_tiles = t_len * TOP_K // tm + n_e
    tile_start = jnp.arange(n_tiles, dtype=I32) * tm
    tile_e = jnp.minimum(jnp.sum((pend[None, :] <= tile_start[:, None]).astype(I32), axis=1), n_e - 1)
    n_active = (pend[-1:] // tm).astype(I32)

    xs = dispatch(h1_packed, dest, tile_e, n_active, n_tiles * tm, tm=tm)
    bgu = b_gate_up.reshape(-1, n_e, b_gate_up.shape[-1] // GU, LANES, 2).swapaxes(-1, -2)
    bgu = bgu.reshape(-1, 1, b_gate_up.shape[-1])
    tile_w = tile_e + layer * n_e
    tidx = jnp.arange(n_tiles, dtype=I32)
    follows = ((tile_w[None, :] != tile_w[:, None]) & (tidx[None, :] > tidx[:, None])
               & (tidx[None, :] < n_active[0]))
    next_e = jnp.where(jnp.any(follows, axis=1), tile_w[jnp.argmax(follows, axis=1)], -1).astype(I32)
    group_end = (pstart + counts)[tile_e]
    n_valid = jnp.clip(group_end - tile_start, 0, tm).astype(I32)
    act = moe_gateup(tile_w, n_active, next_e, n_valid, xs, w_gate_up.reshape((-1,) + w_gate_up.shape[2:]),
                     bgu, tm=tm)
    y = moe_down(tile_w, n_active, next_e, n_valid, act, w_down.reshape((-1,) + w_down.shape[2:]),
                 b_down.reshape(-1, 1, b_down.shape[-1]), tm=tm)
    return combine(y, dest, route, h1, ln_g, ln_b, alpha=alpha)


def kernel(x, mem, a_w_in, a_conv_w, a_conv_b, a_rg_w, a_rg_b, a_ig_w, a_ig_b, a_lambda, a_w_out, b_w_q, b_w_out, w_kv_shared, mem_w_kv, ln1_g, ln1_b, ln2_g, ln2_b, router_w, router_b, w_gate_up, b_gate_up, w_down, b_down):
    bsz, seq, d = x.shape
    depth = ln1_g.shape[0]
    n_a = a_w_in.shape[0]
    alpha = float((2 * depth) ** 0.25)
    outs = []
    for bi in range(bsz):
        h = x[bi]
        mem_b = mem[bi]
        qkv = None
        for layer in range(depth):
            mem_kv = dense(mem_b, mem_w_kv[layer].astype(BF16), BF16, tm=mem_b.shape[0])
            if layer < n_a:
                lw = a_rg_w.shape[1] * LANES
                proj = dense(h, a_w_in[layer].astype(BF16), F32, tm=512)
                main = rglru(proj, a_conv_w[layer], a_conv_b[layer], a_rg_w[layer], a_rg_b[layer],
                             a_ig_w[layer], a_ig_b[layer], a_lambda[layer])
                qsrc, q_col, w_out = proj, 2 * lw // MEM_WIDTH, a_w_out[layer]
            else:
                j = layer - n_a
                sbw = w_kv_shared.shape[1] // 2
                n_heads = sbw // HEAD_DIM
                if j == 0:
                    w_cat = jnp.concatenate([w_kv_shared, b_w_q[j]], axis=1).astype(BF16)
                    qkv = dense(h, w_cat, BF16, tm=512)
                else:
                    q_only = dense(h, b_w_q[j].astype(BF16), BF16, tm=512)
                    qkv = jnp.concatenate([qkv[:, :2 * sbw], q_only], axis=1)
                main = stick_breaking(qkv, n_heads=n_heads, q_col=2 * n_heads, k_col=0, v_col=n_heads)
                qsrc, q_col, w_out = qkv, (3 * sbw) // MEM_WIDTH, b_w_out[j]
            h1, logits, h1_packed = mixout(main, qsrc, q_col, mem_kv, w_out, h, ln1_g[layer], ln1_b[layer],
                                           router_w[layer], router_b[layer], alpha=alpha)
            h = moe(h1, h1_packed, logits, layer, w_gate_up, b_gate_up, w_down, b_down,
                    ln2_g[layer], ln2_b[layer], alpha=alpha)
        outs.append(h)
    return jnp.stack(outs, axis=0)
```

```python
import functools

import jax
import jax.numpy as jnp
from jax import lax
from jax.experimental import pallas as pl
from jax.experimental.pallas import tpu as pltpu

F32 = jnp.float32
BF16 = jnp.bfloat16
I32 = jnp.int32

HEAD_DIM = 128
MEM_HEADS = 4
MEM_WIDTH = MEM_HEADS * HEAD_DIM
CONV_WIDTH = 4
LRU_C = 8.0
N_EXPERTS = 32
TOP_K = 4
SWIGLU_LIMIT = 7.0
SWIGLU_ALPHA = 1.702
LN_EPS = 1e-5
LANES = 128
SUBLANES = 8
GU = 2 * LANES
VMEM_LIMIT = 56 * 1024 * 1024
SB_SKIP = 100.0
SB_HEADS_PER_STEP = 2
ROW_CHUNK = 256
ROW_QUARTERS = 4


def _params(n_axes, vmem=VMEM_LIMIT):
    return pltpu.CompilerParams(dimension_semantics=("arbitrary",) * n_axes, vmem_limit_bytes=vmem)


def _dense_kernel(x_ref, w_ref, o_ref, *, tn):
    xb = x_ref[...].astype(BF16)
    for n in range(0, w_ref.shape[1], tn):
        o_ref[:, n:n + tn] = jnp.dot(xb, w_ref[:, n:n + tn], preferred_element_type=F32).astype(o_ref.dtype)


def dense(x, w, out_dtype, *, tm, tn=256):
    m, k = x.shape
    n = w.shape[1]
    return pl.pallas_call(
        functools.partial(_dense_kernel, tn=tn),
        out_shape=jax.ShapeDtypeStruct((m, n), out_dtype),
        grid=(m // tm,),
        in_specs=[pl.BlockSpec((tm, k), lambda i: (i, 0)),
                  pl.BlockSpec((k, n), lambda i: (0, 0), pipeline_mode=pl.Buffered(1))],
        out_specs=pl.BlockSpec((tm, n), lambda i: (i, 0)),
        compiler_params=_params(1),
        name="dense",
    )(x, w)


def _softplus(x):
    return jnp.maximum(x, 0.0) + jnp.log1p(jnp.exp(-jnp.abs(x)))


def _gelu_tanh(x):
    return 0.5 * x * (1.0 + jnp.tanh(0.7978845608028654 * (x + 0.044715 * (x * x * x))))


def _rglru_kernel(gate_ref, u_ref, cw_ref, cb_ref, rgw_ref, rgb_ref, igw_ref, igb_ref, lam_ref,
                  o_ref, hist_ref, carry_ref, *, tt):
    t = pl.program_id(1)

    @pl.when(t == 0)
    def _():
        hist_ref[...] = jnp.zeros_like(hist_ref)
        carry_ref[...] = jnp.zeros_like(carry_ref)

    u = u_ref[...]
    ext = jnp.concatenate([hist_ref[...], u], axis=0)
    cw = cw_ref[...]
    xc = cb_ref[...] + u * cw[CONV_WIDTH - 1:CONV_WIDTH, :]
    for d in range(1, CONV_WIDTH):
        shifted = pltpu.roll(ext, d, axis=0)[SUBLANES:, :]
        xc = xc + shifted * cw[CONV_WIDTH - 1 - d:CONV_WIDTH - d, :]
    hist_ref[...] = u[tt - SUBLANES:, :]

    xcb = xc.astype(BF16)
    r = jax.nn.sigmoid(jnp.dot(xcb, rgw_ref[...], preferred_element_type=F32) + rgb_ref[...])
    ig = jax.nn.sigmoid(jnp.dot(xcb, igw_ref[...], preferred_element_type=F32) + igb_ref[...])
    log_a = (-LRU_C) * r * _softplus(-lam_ref[...])
    a = jnp.exp(log_a)
    b = jnp.sqrt(-jnp.tanh(log_a) * (a * a + 1.0)) * (ig * xc)

    groups = tt // SUBLANES
    a = a.reshape(groups, SUBLANES, LANES)
    b = b.reshape(groups, SUBLANES, LANES)
    sub = lax.broadcasted_iota(I32, (groups, SUBLANES, LANES), 1)
    d = 1
    while d < SUBLANES:
        keep = sub >= d
        a_sh = jnp.where(keep, pltpu.roll(a, d, axis=1), 1.0)
        b_sh = jnp.where(keep, pltpu.roll(b, d, axis=1), 0.0)
        b = a * b_sh + b
        a = a * a_sh
        d *= 2
    a = a.reshape(tt, LANES)
    b = b.reshape(tt, LANES)
    carry = carry_ref[...]
    hs = []
    for g in range(tt // SUBLANES):
        rows = slice(g * SUBLANES, (g + 1) * SUBLANES)
        hs.append(a[rows, :] * carry + b[rows, :])
        carry = hs[-1][SUBLANES - 1:SUBLANES, :]
    carry_ref[...] = carry
    o_ref[...] = (jnp.concatenate(hs, axis=0) * _gelu_tanh(gate_ref[...])).astype(o_ref.dtype)


def rglru(proj, conv_w, conv_b, rg_w, rg_b, ig_w, ig_b, lam, *, tt=512):
    t_len = proj.shape[0]
    nblk = rg_w.shape[0]
    width = nblk * LANES
    vec = lambda v: v.reshape(1, width)
    chan = lambda rows: pl.BlockSpec((rows, LANES), lambda n, t: (0, n))
    return pl.pallas_call(
        functools.partial(_rglru_kernel, tt=tt),
        out_shape=jax.ShapeDtypeStruct((t_len, width), BF16),
        grid=(nblk, t_len // tt),
        in_specs=[pl.BlockSpec((tt, LANES), lambda n, t: (t, n)),
                  pl.BlockSpec((tt, LANES), lambda n, t: (t, nblk + n)),
                  chan(CONV_WIDTH), chan(1),
                  pl.BlockSpec((None, LANES, LANES), lambda n, t: (n, 0, 0)), chan(1),
                  pl.BlockSpec((None, LANES, LANES), lambda n, t: (n, 0, 0)), chan(1),
                  chan(1)],
        out_specs=pl.BlockSpec((tt, LANES), lambda n, t: (t, n)),
        scratch_shapes=[pltpu.VMEM((SUBLANES, LANES), F32), pltpu.VMEM((1, LANES), F32)],
        compiler_params=_params(2),
        name="rglru",
    )(proj, proj, conv_w, vec(conv_b), rg_w.astype(BF16), vec(rg_b), ig_w.astype(BF16), vec(ig_b), vec(lam))


def _sb_kernel(q_ref, k_ref, v_ref, o_ref, acc_ref, run_ref, *, tq):
    i = pl.program_id(1)
    scale = HEAD_DIM ** -0.5
    rows = lax.broadcasted_iota(I32, (tq, tq), 0)
    cols = lax.broadcasted_iota(I32, (tq, tq), 1)
    later = (rows > cols).astype(BF16)
    causal = cols < rows
    heads = range(SB_HEADS_PER_STEP)

    def rows_of(j):
        return pl.ds(j * tq if isinstance(j, int) else pl.multiple_of(j * tq, tq), tq)

    def lanes_of(hd):
        return slice(hd * HEAD_DIM, (hd + 1) * HEAD_DIM)

    def front(hd, j, masked):
        z = lax.dot_general(q_ref[:, lanes_of(hd)], k_ref[rows_of(j), lanes_of(hd)], (((1,), (1,)), ((), ())),
                            preferred_element_type=F32) * scale
        sp = jnp.maximum(z, 0.0) + jnp.log(1.0 + jnp.exp(-jnp.abs(z)))
        log_keep = -sp
        log_beta = z - sp
        if masked:
            log_keep = jnp.where(causal, log_keep, 0.0)
        hi = log_keep.astype(BF16)
        lo = (log_keep - hi.astype(F32)).astype(BF16)
        after = (jnp.dot(hi, later, preferred_element_type=F32)
                 + jnp.dot(lo, later, preferred_element_type=F32))
        return log_beta + after, after[:, 0:1] + log_keep[:, 0:1]

    def back(hd, j, logw, run, masked):
        w = jnp.exp(logw if run is None else logw + run)
        if masked:
            w = jnp.where(causal, w, 0.0)
        return jnp.dot(w.astype(BF16), v_ref[rows_of(j), lanes_of(hd)], preferred_element_type=F32)

    @pl.when(i == 0)
    def _():
        for hd in heads:
            logw, _ = front(hd, 0, True)
            o_ref[:, lanes_of(hd)] = back(hd, 0, logw, None, True).astype(o_ref.dtype)

    @pl.when(i > 0)
    def _():
        tops = []
        for hd in heads:
            logw_d, tot_d = front(hd, i, True)
            logw_p, tot_p = front(hd, i - 1, False)
            acc_ref[:, lanes_of(hd)] = back(hd, i, logw_d, None, True) + back(hd, i - 1, logw_p, tot_d, False)
            run0 = tot_d + tot_p
            run_ref[hd] = run0
            tops.append(jnp.max(run0))

        def cond(state):
            j, top = state
            return jnp.logical_and(j >= 0, top > -SB_SKIP)

        def body(state):
            j, _ = state
            tops = []
            for hd in heads:
                logw, tot = front(hd, j, False)
                run = run_ref[hd]
                acc_ref[:, lanes_of(hd)] += back(hd, j, logw, run, False)
                run = run + tot
                run_ref[hd] = run
                tops.append(jnp.max(run))
            return j - 1, functools.reduce(jnp.maximum, tops)

        lax.while_loop(cond, body, (i - 2, functools.reduce(jnp.maximum, tops)))
        o_ref[...] = acc_ref[...].astype(o_ref.dtype)


def stick_breaking(qkv, *, n_heads, q_col, k_col, v_col, tq=256):
    t_len = qkv.shape[0]
    hg = SB_HEADS_PER_STEP
    width = hg * HEAD_DIM
    assert n_heads % hg == 0 and q_col % hg == 0 and k_col % hg == 0 and v_col % hg == 0
    return pl.pallas_call(
        functools.partial(_sb_kernel, tq=tq),
        out_shape=jax.ShapeDtypeStruct((t_len, n_heads * HEAD_DIM), BF16),
        grid=(n_heads // hg, t_len // tq),
        in_specs=[pl.BlockSpec((tq, width), lambda h, i: (i, q_col // hg + h)),
                  pl.BlockSpec((t_len, width), lambda h, i: (0, k_col // hg + h)),
                  pl.BlockSpec((t_len, width), lambda h, i: (0, v_col // hg + h))],
        out_specs=pl.BlockSpec((tq, width), lambda h, i: (i, h)),
        scratch_shapes=[pltpu.VMEM((tq, width), F32), pltpu.VMEM((hg, tq, 1), F32)],
        compiler_params=_params(2),
        name="stick_breaking",
    )(qkv, qkv, qkv)


def _layer_norm(y, g, b):
    mu = jnp.mean(y, axis=-1, keepdims=True)
    yc = y - mu
    var = jnp.mean(yc * yc, axis=-1, keepdims=True)
    return yc * lax.rsqrt(var + LN_EPS) * g + b


def _mixout_kernel(main_ref, qm_ref, mkv_ref, wout_ref, h_ref, g_ref, b_ref, rw2_ref, rb_ref,
                   o_ref, logit_ref, pack_ref, *, alpha):
    main_w = main_ref.shape[1]
    mix = jnp.dot(main_ref[...], wout_ref[0:main_w, :], preferred_element_type=F32)
    q = qm_ref[...].astype(BF16)
    heads = []
    for hd in range(MEM_HEADS):
        lo, hi = hd * HEAD_DIM, (hd + 1) * HEAD_DIM
        kh = mkv_ref[:, lo:hi]
        vh = mkv_ref[:, MEM_WIDTH + lo:MEM_WIDTH + hi]
        s = lax.dot_general(q[:, lo:hi], kh, (((1,), (1,)), ((), ())),
                            preferred_element_type=F32) * (HEAD_DIM ** -0.5)
        e = jnp.exp(s - jnp.max(s, axis=-1, keepdims=True))
        p = e * (1.0 / jnp.sum(e, axis=-1, keepdims=True))
        heads.append(jnp.dot(p.astype(BF16), vh, preferred_element_type=F32).astype(BF16))
    mem_out = jnp.concatenate(heads, axis=-1)
    mix = mix + jnp.dot(mem_out, wout_ref[main_w:, :], preferred_element_type=F32)
    h1 = _layer_norm(alpha * h_ref[...] + mix, g_ref[...], b_ref[...])
    o_ref[...] = h1
    hh = h1.astype(BF16)
    hh32 = hh.astype(F32)
    hl = (h1 - hh32).astype(BF16)
    both = jnp.dot(hh, rw2_ref[...], preferred_element_type=F32)
    logit_ref[...] = (both[:, :LANES] + both[:, LANES:]
                      + jnp.dot(hl, rw2_ref[:, :LANES], preferred_element_type=F32) + rb_ref[...])
    bits = pltpu.bitcast(hh32, jnp.uint32)
    half = bits.shape[1] // 2
    packed = bits[:, :half] | (bits[:, half:] >> 16)
    pack_ref[...] = packed.reshape(packed.shape[0], half // LANES, LANES)


def mixout(main, qsrc, q_col, mem_kv, w_out, h, ln_g, ln_b, router_w, router_b, *, alpha, tm=512):
    t_len, d = h.shape
    main_w = main.shape[1]
    n_e = router_w.shape[1]
    rw = jnp.zeros((d, LANES), F32).at[:, :n_e].set(router_w)
    rwh = rw.astype(BF16)
    rw2 = jnp.concatenate([rwh, (rw - rwh.astype(F32)).astype(BF16)], axis=1)
    rb = jnp.full((1, LANES), -jnp.inf, F32).at[0, :n_e].set(router_b)
    full = lambda a: pl.BlockSpec(a.shape, lambda i: (0, 0))
    args = (main, qsrc, mem_kv, w_out.astype(BF16), h, ln_g.reshape(1, d), ln_b.reshape(1, d), rw2, rb)
    return pl.pallas_call(
        functools.partial(_mixout_kernel, alpha=alpha),
        out_shape=(jax.ShapeDtypeStruct((t_len, d), F32), jax.ShapeDtypeStruct((t_len, LANES), F32),
                   jax.ShapeDtypeStruct((t_len, d // 2 // LANES, LANES), jnp.uint32)),
        grid=(t_len // tm,),
        in_specs=[pl.BlockSpec((tm, main_w), lambda i: (i, 0)),
                  pl.BlockSpec((tm, MEM_WIDTH), lambda i: (i, q_col)),
                  full(mem_kv), full(args[3]),
                  pl.BlockSpec((tm, d), lambda i: (i, 0)),
                  full(args[5]), full(args[6]), full(rw2), full(rb)],
        out_specs=(pl.BlockSpec((tm, d), lambda i: (i, 0)), pl.BlockSpec((tm, LANES), lambda i: (i, 0)),
                   pl.BlockSpec((tm, d // 2 // LANES, LANES), lambda i: (i, 0, 0))),
        compiler_params=_params(1),
        name="mixout",
    )(*args)


def _router_kernel(logit_ref, route_ref, cnt_ref, carry_ref, *, tb):
    @pl.when(pl.program_id(0) == 0)
    def _():
        carry_ref[...] = jnp.zeros_like(carry_ref)

    vals = logit_ref[...]
    lane = lax.broadcasted_iota(I32, (tb, LANES), 1).astype(F32)
    sels, tops, ids = [], [], []
    for _ in range(TOP_K):
        m = jnp.max(vals, axis=-1, keepdims=True)
        idx = jnp.min(jnp.where(vals == m, lane, float(LANES)), axis=-1, keepdims=True)
        sel = lane == idx
        vals = jnp.where(sel, -jnp.inf, vals)
        sels.append(sel)
        tops.append(m)
        ids.append(idx)
    exps = [jnp.exp(m - tops[0]) for m in tops]
    inv = 1.0 / (exps[0] + exps[1] + exps[2] + exps[3])
    onehot = jnp.zeros((tb, LANES), F32)
    for sel in sels:
        onehot = onehot + sel.astype(F32)
    rows = lax.broadcasted_iota(I32, (tb, tb), 0)
    cols = lax.broadcasted_iota(I32, (tb, tb), 1)
    before = (cols < rows).astype(BF16)
    cum = jnp.dot(before, onehot.astype(BF16), preferred_element_type=F32) + carry_ref[...]
    out = jnp.zeros((tb, LANES), F32)
    for k in range(TOP_K):
        pos = jnp.sum(jnp.where(sels[k], cum, 0.0), axis=-1, keepdims=True)
        out = jnp.where(lane == k, ids[k], out)
        out = jnp.where(lane == TOP_K + k, exps[k] * inv, out)
        out = jnp.where(lane == 2 * TOP_K + k, pos, out)
    route_ref[...] = out
    carry_ref[...] += jnp.sum(onehot, axis=0, keepdims=True)
    cnt_ref[...] = jnp.broadcast_to(carry_ref[...], cnt_ref.shape)


def router(logits, *, tb=512):
    t_len = logits.shape[0]
    return pl.pallas_call(
        functools.partial(_router_kernel, tb=tb),
        out_shape=(jax.ShapeDtypeStruct((t_len, LANES), F32), jax.ShapeDtypeStruct((SUBLANES, LANES), F32)),
        grid=(t_len // tb,),
        in_specs=[pl.BlockSpec((tb, LANES), lambda i: (i, 0))],
        out_specs=(pl.BlockSpec((tb, LANES), lambda i: (i, 0)), pl.BlockSpec((SUBLANES, LANES), lambda i: (0, 0))),
        scratch_shapes=[pltpu.VMEM((1, LANES), F32)],
        compiler_params=_params(1),
        name="router",
    )(logits)


def _dispatch_kernel(te_ref, na_ref, dest_ref, h_ref, xs_ref, zero_ref, sem, zsem, *, tb, tm, n_tiles):
    @pl.when(pl.program_id(0) == 0)
    def _():
        zero_ref[...] = jnp.zeros_like(zero_ref)

        def has_padding(j):
            return jnp.logical_or(j >= na_ref[0] - 1, te_ref[j] != te_ref[jnp.minimum(j + 1, n_tiles - 1)])

        def zero_copy(j):
            return pltpu.make_async_copy(zero_ref, xs_ref.at[pl.ds(pl.multiple_of(j * tm, tm), tm)], zsem)

        def zstart(j, c):
            @pl.when(has_padding(j))
            def _():
                zero_copy(j).start()
            return c

        def zwait(j, c):
            @pl.when(has_padding(j))
            def _():
                zero_copy(j).wait()
            return c

        lax.fori_loop(0, n_tiles, zstart, 0)
        lax.fori_loop(0, n_tiles, zwait, 0)

    def start(r, c):
        for k in range(TOP_K):
            slot = dest_ref[0, 0, r * TOP_K + k]
            pltpu.make_async_copy(h_ref.at[pl.ds(r, 1)], xs_ref.at[pl.ds(slot, 1)], sem).start(priority=k % 2)
        return c

    lax.fori_loop(0, tb, start, 0)
    rows = xs_ref.at[pl.ds(0, tb * TOP_K)]
    pltpu.make_async_copy(rows, rows, sem).wait()


def dispatch(h, dest, tile_e, n_active, n_rows, *, tm, tb=256):
    t_len, slab = h.shape[0], h.shape[1:]
    dest3 = dest.reshape(t_len // tb, 1, tb * TOP_K)
    return pl.pallas_call(
        functools.partial(_dispatch_kernel, tb=tb, tm=tm, n_tiles=n_rows // tm),
        out_shape=jax.ShapeDtypeStruct((n_rows,) + slab, h.dtype),
        grid_spec=pltpu.PrefetchScalarGridSpec(
            num_scalar_prefetch=2,
            grid=(t_len // tb,),
            in_specs=[pl.BlockSpec((1, 1, tb * TOP_K), lambda i, te, na: (i, 0, 0), memory_space=pltpu.SMEM),
                      pl.BlockSpec((tb,) + slab, lambda i, te, na: (i, 0, 0))],
            out_specs=pl.BlockSpec(memory_space=pl.ANY),
            scratch_shapes=[pltpu.VMEM((tm,) + slab, h.dtype), pltpu.SemaphoreType.DMA(()),
                            pltpu.SemaphoreType.DMA(())]),
        compiler_params=_params(1),
        name="dispatch",
    )(tile_e, n_active, dest3, h)


def _tile_idx(i, te_ref, na_ref):
    return jnp.minimum(i, na_ref[0] - 1)


def _zero_inactive(i, na_ref, o_ref):
    @pl.when(i >= na_ref[0])
    def _():
        o_ref[...] = jnp.zeros_like(o_ref)


def _expert_changed(i, te_ref):
    return jnp.logical_or(i == 0, te_ref[i] != te_ref[jnp.maximum(i - 1, 0)])


def _gateup_kernel(te_ref, na_ref, nx_ref, nv_ref, x_ref, w_hbm, b_ref, o_ref, stage_ref, wb_ref, xb_ref, sem):
    i = pl.program_id(0)
    nh = stage_ref.shape[1]
    _zero_inactive(i, na_ref, o_ref)

    def wcopy(e, hf):
        return pltpu.make_async_copy(w_hbm.at[e, :, pl.ds(hf * nh, nh)], stage_ref, sem)

    def reorder(hf):
        rows = lax.broadcasted_iota(I32, (GU, GU), 0)
        cols = lax.broadcasted_iota(I32, (GU, GU), 1)
        src = jnp.where(cols < LANES, 2 * cols, 2 * (cols - LANES) + 1)
        perm = (rows == src).astype(BF16)
        for n in range(0, nh, GU):
            wb_ref[:, hf * nh + n:hf * nh + n + GU] = jnp.dot(
                stage_ref[:, n:n + GU].astype(BF16), perm, preferred_element_type=F32).astype(BF16)

    def columns_of(hf, rows):
        x = xb_ref[0:rows, :]
        for n in range(hf * nh, (hf + 1) * nh, GU):
            gu = jnp.dot(x, wb_ref[:, n:n + GU], preferred_element_type=F32) + b_ref[:, n:n + GU]
            gate = jnp.minimum(gu[:, :LANES], SWIGLU_LIMIT)
            up = jnp.clip(gu[:, LANES:], -SWIGLU_LIMIT, SWIGLU_LIMIT)
            act = (up + 1.0) * gate * jax.nn.sigmoid(SWIGLU_ALPHA * gate)
            o_ref[0:rows, n // 2:n // 2 + LANES] = act.astype(o_ref.dtype)

    def columns(hf):
        tm = o_ref.shape[0]
        quarter = tm // ROW_QUARTERS
        used = (nv_ref[i] + quarter - 1) // quarter
        for q in range(1, ROW_QUARTERS + 1):
            @pl.when(used == q if q > 1 else used <= 1)
            def _():
                columns_of(hf, q * quarter)
                if q < ROW_QUARTERS:
                    o_ref[q * quarter:, hf * nh // 2:(hf + 1) * nh // 2] = jnp.zeros(
                        (tm - q * quarter, nh // 2), o_ref.dtype)

    @pl.when(i < na_ref[0])
    def _():
        changed = _expert_changed(i, te_ref)
        e = te_ref[i]
        words = x_ref[...].reshape(x_ref.shape[0], -1)
        upper = pltpu.bitcast(words & jnp.uint32(0xFFFF0000), F32)
        lower = pltpu.bitcast(words << 16, F32)
        xb_ref[...] = jnp.concatenate([upper, lower], axis=1).astype(BF16)

        @pl.when(i == 0)
        def _():
            wcopy(e, 0).start()

        @pl.when(changed)
        def _():
            wcopy(e, 0).wait()
            reorder(0)
            wcopy(e, 1).start()

        columns(0)

        @pl.when(changed)
        def _():
            wcopy(e, 1).wait()
            reorder(1)

            @pl.when(nx_ref[i] >= 0)
            def _():
                wcopy(nx_ref[i], 0).start()

        columns(1)


def moe_gateup(tile_e, n_active, next_e, n_valid, x, w, b, *, tm):
    n_rows = x.shape[0]
    k, n2 = w.shape[1], w.shape[2]
    return pl.pallas_call(
        _gateup_kernel,
        out_shape=jax.ShapeDtypeStruct((n_rows, n2 // 2), BF16),
        grid_spec=pltpu.PrefetchScalarGridSpec(
            num_scalar_prefetch=4,
            grid=(n_rows // tm,),
            in_specs=[pl.BlockSpec((tm,) + x.shape[1:], lambda i, te, na, nx, nv: (_tile_idx(i, te, na), 0, 0)),
                      pl.BlockSpec(memory_space=pl.ANY),
                      pl.BlockSpec((None, 1, n2), lambda i, te, na, nx, nv: (te[_tile_idx(i, te, na)], 0, 0))],
            out_specs=pl.BlockSpec((tm, n2 // 2), lambda i, te, na, nx, nv: (i, 0)),
            scratch_shapes=[pltpu.VMEM((k, n2 // 2), F32), pltpu.VMEM((k, n2), BF16),
                            pltpu.VMEM((tm, k), BF16), pltpu.SemaphoreType.DMA(())]),
        compiler_params=_params(1),
        name="moe_gateup",
    )(tile_e, n_active, next_e, n_valid, x, w, b)


def _down_kernel(te_ref, na_ref, nx_ref, nv_ref, x_ref, w_hbm, b_ref, o_ref, stage_ref, wb_ref, sem, *, tn):
    i = pl.program_id(0)
    _zero_inactive(i, na_ref, o_ref)

    def wcopy(e):
        return pltpu.make_async_copy(w_hbm.at[e], stage_ref, sem)

    @pl.when(i < na_ref[0])
    def _():
        e = te_ref[i]

        @pl.when(i == 0)
        def _():
            wcopy(e).start()

        @pl.when(_expert_changed(i, te_ref))
        def _():
            wcopy(e).wait()
            wb_ref[...] = stage_ref[...].astype(BF16)

            @pl.when(nx_ref[i] >= 0)
            def _():
                wcopy(nx_ref[i]).start()

        def rows_of(rows):
            for r0 in range(0, rows, ROW_CHUNK):
                rsub = min(ROW_CHUNK, rows - r0)
                x = x_ref[r0:r0 + rsub, :]
                for n in range(0, wb_ref.shape[1], tn):
                    y = jnp.dot(x, wb_ref[:, n:n + tn], preferred_element_type=F32) + b_ref[:, n:n + tn]
                    o_ref[r0:r0 + rsub, n // LANES:(n + tn) // LANES, :] = y.reshape(rsub, tn // LANES, LANES)

        tm = o_ref.shape[0]
        quarter = tm // ROW_QUARTERS
        used = (nv_ref[i] + quarter - 1) // quarter
        for q in range(1, ROW_QUARTERS + 1):
            @pl.when(used == q if q > 1 else used <= 1)
            def _():
                rows_of(q * quarter)
                if q < ROW_QUARTERS:
                    o_ref[q * quarter:] = jnp.zeros((tm - q * quarter,) + o_ref.shape[1:], o_ref.dtype)


def moe_down(tile_e, n_active, next_e, n_valid, x, w, b, *, tm, tn=SUBLANES * LANES):
    n_rows, k = x.shape
    n_out = w.shape[2]
    return pl.pallas_call(
        functools.partial(_down_kernel, tn=tn),
        out_shape=jax.ShapeDtypeStruct((n_rows, n_out // LANES, LANES), F32),
        grid_spec=pltpu.PrefetchScalarGridSpec(
            num_scalar_prefetch=4,
            grid=(n_rows // tm,),
            in_specs=[pl.BlockSpec((tm, k), lambda i, te, na, nx, nv: (_tile_idx(i, te, na), 0)),
                      pl.BlockSpec(memory_space=pl.ANY),
                      pl.BlockSpec((None, 1, n_out), lambda i, te, na, nx, nv: (te[_tile_idx(i, te, na)], 0, 0))],
            out_specs=pl.BlockSpec((tm, n_out // LANES, LANES), lambda i, te, na, nx, nv: (i, 0, 0)),
            scratch_shapes=[pltpu.VMEM((k, n_out), F32), pltpu.VMEM((k, n_out), BF16),
                            pltpu.SemaphoreType.DMA(())]),
        compiler_params=_params(1),
        name="moe_down",
    )(tile_e, n_active, next_e, n_valid, x, w, b)


def _combine_kernel(dest_ref, next_ref, gate_ref, h_ref, g_ref, b_ref, y_ref, o_ref, buf_ref, sems, *, tb, alpha):
    i = pl.program_id(0)

    def gather(idx_ref, s):
        def start(r, c):
            for k in range(TOP_K):
                row = idx_ref[0, 0, r * TOP_K + k]
                pltpu.make_async_copy(y_ref.at[pl.ds(row, 1)], buf_ref.at[s, k, pl.ds(r, 1)],
                                      sems.at[s]).start(priority=k % 2)
            return c
        lax.fori_loop(0, tb, start, 0)

    @pl.when(i == 0)
    def _():
        gather(dest_ref, 0)

    for s in range(2):
        @pl.when(i % 2 == s)
        def _():
            @pl.when(i + 1 < pl.num_programs(0))
            def _():
                gather(next_ref, 1 - s)

            pltpu.make_async_copy(buf_ref.at[s], buf_ref.at[s], sems.at[s]).wait()
            gates = gate_ref[...]
            d = h_ref.shape[1]
            ffn = buf_ref[s, 0].reshape(tb, d) * gates[:, TOP_K:TOP_K + 1]
            for k in range(1, TOP_K):
                ffn = ffn + buf_ref[s, k].reshape(tb, d) * gates[:, TOP_K + k:TOP_K + k + 1]
            o_ref[...] = _layer_norm(alpha * h_ref[...] + ffn, g_ref[...], b_ref[...])


def combine(y, dest, route, h, ln_g, ln_b, *, alpha, tb=256):
    t_len, d = h.shape
    nb = t_len // tb
    dest3 = dest.reshape(nb, 1, tb * TOP_K)
    idx_spec = lambda f: pl.BlockSpec((1, 1, tb * TOP_K), f, memory_space=pltpu.SMEM)
    return pl.pallas_call(
        functools.partial(_combine_kernel, tb=tb, alpha=alpha),
        out_shape=jax.ShapeDtypeStruct((t_len, d), F32),
        grid=(nb,),
        in_specs=[idx_spec(lambda i: (i, 0, 0)),
                  idx_spec(lambda i: (jnp.minimum(i + 1, nb - 1), 0, 0)),
                  pl.BlockSpec((tb, LANES), lambda i: (i, 0)),
                  pl.BlockSpec((tb, d), lambda i: (i, 0)),
                  pl.BlockSpec((1, d), lambda i: (0, 0)),
                  pl.BlockSpec((1, d), lambda i: (0, 0)),
                  pl.BlockSpec(memory_space=pl.ANY)],
        out_specs=pl.BlockSpec((tb, d), lambda i: (i, 0)),
        scratch_shapes=[pltpu.VMEM((2, TOP_K, tb) + y.shape[1:], F32), pltpu.SemaphoreType.DMA((2,))],
        compiler_params=_params(1),
        name="combine",
    )(dest3, dest3, route, h, ln_g.reshape(1, d), ln_b.reshape(1, d), y)


def moe(h1, h1_packed, logits, layer, w_gate_up, b_gate_up, w_down, b_down, ln_g, ln_b, *, alpha, tm=512):
    t_len, d = h1.shape
    n_e = w_gate_up.shape[1]
    route, cnt = router(logits)
    eid = route[:, 0:TOP_K].astype(I32)
    pos = route[:, 2 * TOP_K:3 * TOP_K].astype(I32)
    counts = cnt[0, :n_e].astype(I32)
    padded = (counts + tm - 1) // tm * tm
    pend = jnp.cumsum(padded)
    pstart = pend - padded
    onehot = eid[:, :, None] == jnp.arange(n_e, dtype=I32)[None, None, :]
    dest = jnp.sum(jnp.where(onehot, pstart[None, None, :], 0), axis=-1) + pos
    n_tiles = t_len * TOP_K // tm + n_e
    tile_start = jnp.arange(n_tiles, dtype=I32) * tm
    tile_e = jnp.minimum(jnp.sum((pend[None, :] <= tile_start[:, None]).astype(I32), axis=1), n_e - 1)
    n_active = (pend[-1:] // tm).astype(I32)

    xs = dispatch(h1_packed, dest, tile_e, n_active, n_tiles * tm, tm=tm)
    bgu = b_gate_up.reshape(-1, n_e, b_gate_up.shape[-1] // GU, LANES, 2).swapaxes(-1, -2)
    bgu = bgu.reshape(-1, 1, b_gate_up.shape[-1])
    tile_w = tile_e + layer * n_e
    tidx = jnp.arange(n_tiles, dtype=I32)
    follows = ((tile_w[None, :] != tile_w[:, None]) & (tidx[None, :] > tidx[:, None])
               & (tidx[None, :] < n_active[0]))
    next_e = jnp.where(jnp.any(follows, axis=1), tile_w[jnp.argmax(follows, axis=1)], -1).astype(I32)
    group_end = (pstart + counts)[tile_e]
    n_valid = jnp.clip(group_end - tile_start, 0, tm).astype(I32)
    act = moe_gateup(tile_w, n_active, next_e, n_valid, xs, w_gate_up.reshape((-1,) + w_gate_up.shape[2:]),
                     bgu, tm=tm)
    y = moe_down(tile_w, n_active, next_e, n_valid, act, w_down.reshape((-1,) + w_down.shape[2:]),
                 b_down.reshape(-1, 1, b_down.shape[-1]), tm=tm)
    return combine(y, dest, route, h1, ln_g, ln_b, alpha=alpha)


def kernel(x, mem, a_w_in, a_conv_w, a_conv_b, a_rg_w, a_rg_b, a_ig_w, a_ig_b, a_lambda, a_w_out, b_w_q, b_w_out, w_kv_shared, mem_w_kv, ln1_g, ln1_b, ln2_g, ln2_b, router_w, router_b, w_gate_up, b_gate_up, w_down, b_down):
    bsz, seq, d = x.shape
    depth = ln1_g.shape[0]
    n_a = a_w_in.shape[0]
    alpha = float((2 * depth) ** 0.25)
    outs = []
    for bi in range(bsz):
        h = x[bi]
        mem_b = mem[bi]
        qkv = None
        for layer in range(depth):
            mem_kv = dense(mem_b, mem_w_kv[layer].astype(BF16), BF16, tm=mem_b.shape[0])
            if layer < n_a:
                lw = a_rg_w.shape[1] * LANES
                proj = dense(h, a_w_in[layer].astype(BF16), F32, tm=512)
                main = rglru(proj, a_conv_w[layer], a_conv_b[layer], a_rg_w[layer], a_rg_b[layer],
                             a_ig_w[layer], a_ig_b[layer], a_lambda[layer])
                qsrc, q_col, w_out = proj, 2 * lw // MEM_WIDTH, a_w_out[layer]
            else:
                j = layer - n_a
                sbw = w_kv_shared.shape[1] // 2
                n_heads = sbw // HEAD_DIM
                if j == 0:
                    w_cat = jnp.concatenate([w_kv_shared, b_w_q[j]], axis=1).astype(BF16)
                    qkv = dense(h, w_cat, BF16, tm=512)
                else:
                    q_only = dense(h, b_w_q[j].astype(BF16), BF16, tm=512)
                    qkv = jnp.concatenate([qkv[:, :2 * sbw], q_only], axis=1)
                main = stick_breaking(qkv, n_heads=n_heads, q_col=2 * n_heads, k_col=0, v_col=n_heads)
                qsrc, q_col, w_out = qkv, (3 * sbw) // MEM_WIDTH, b_w_out[j]
            h1, logits, h1_packed = mixout(main, qsrc, q_col, mem_kv, w_out, h, ln1_g[layer], ln1_b[layer],
                                           router_w[layer], router_b[layer], alpha=alpha)
            h = moe(h1, h1_packed, logits, layer, w_gate_up, b_gate_up, w_down, b_down,
                    ln2_g[layer], ln2_b[layer], alpha=alpha)
        outs.append(h)
    return jnp.stack(outs, axis=0)
```

```python
import functools

import jax
import jax.numpy as jnp
from jax import lax
from jax.experimental import pallas as pl
from jax.experimental.pallas import tpu as pltpu

F32 = jnp.float32
BF16 = jnp.bfloat16
I32 = jnp.int32

HEAD_DIM = 128
MEM_HEADS = 4
MEM_WIDTH = MEM_HEADS * HEAD_DIM
CONV_WIDTH = 4
LRU_C = 8.0
N_EXPERTS = 32
TOP_K = 4
SWIGLU_LIMIT = 7.0
SWIGLU_ALPHA = 1.702
LN_EPS = 1e-5
LANES = 128
SUBLANES = 8
GU = 2 * LANES
VMEM_LIMIT = 56 * 1024 * 1024
SB_SKIP = 100.0
SB_HEADS_PER_STEP = 2
ROW_CHUNK = 256
ROW_QUARTERS = 4


def _params(n_axes, vmem=VMEM_LIMIT):
    return pltpu.CompilerParams(dimension_semantics=("arbitrary",) * n_axes, vmem_limit_bytes=vmem)


def _dense_kernel(x_ref, w_ref, o_ref, *, tn):
    xb = x_ref[...].astype(BF16)
    for n in range(0, w_ref.shape[1], tn):
        o_ref[:, n:n + tn] = jnp.dot(xb, w_ref[:, n:n + tn], preferred_element_type=F32).astype(o_ref.dtype)


def dense(x, w, out_dtype, *, tm, tn=256):
    m, k = x.shape
    n = w.shape[1]
    return pl.pallas_call(
        functools.partial(_dense_kernel, tn=tn),
        out_shape=jax.ShapeDtypeStruct((m, n), out_dtype),
        grid=(m // tm,),
        in_specs=[pl.BlockSpec((tm, k), lambda i: (i, 0)),
                  pl.BlockSpec((k, n), lambda i: (0, 0), pipeline_mode=pl.Buffered(1))],
        out_specs=pl.BlockSpec((tm, n), lambda i: (i, 0)),
        compiler_params=_params(1),
        name="dense",
    )(x, w)


def _softplus(x):
    return jnp.maximum(x, 0.0) + jnp.log1p(jnp.exp(-jnp.abs(x)))


def _gelu_tanh(x):
    return 0.5 * x * (1.0 + jnp.tanh(0.7978845608028654 * (x + 0.044715 * (x * x * x))))


def _rglru_kernel(gate_ref, u_ref, cw_ref, cb_ref, rgw_ref, rgb_ref, igw_ref, igb_ref, lam_ref,
                  o_ref, hist_ref, carry_ref, *, tt):
    t = pl.program_id(1)

    @pl.when(t == 0)
    def _():
        hist_ref[...] = jnp.zeros_like(hist_ref)
        carry_ref[...] = jnp.zeros_like(carry_ref)

    u = u_ref[...]
    ext = jnp.concatenate([hist_ref[...], u], axis=0)
    cw = cw_ref[...]
    xc = cb_ref[...] + u * cw[CONV_WIDTH - 1:CONV_WIDTH, :]
    for d in range(1, CONV_WIDTH):
        shifted = pltpu.roll(ext, d, axis=0)[SUBLANES:, :]
        xc = xc + shifted * cw[CONV_WIDTH - 1 - d:CONV_WIDTH - d, :]
    hist_ref[...] = u[tt - SUBLANES:, :]

    xcb = xc.astype(BF16)
    r = jax.nn.sigmoid(jnp.dot(xcb, rgw_ref[...], preferred_element_type=F32) + rgb_ref[...])
    ig = jax.nn.sigmoid(jnp.dot(xcb, igw_ref[...], preferred_element_type=F32) + igb_ref[...])
    log_a = (-LRU_C) * r * _softplus(-lam_ref[...])
    a = jnp.exp(log_a)
    b = jnp.sqrt(-jnp.tanh(log_a) * (a * a + 1.0)) * (ig * xc)

    groups = tt // SUBLANES
    a = a.reshape(groups, SUBLANES, LANES)
    b = b.reshape(groups, SUBLANES, LANES)
    sub = lax.broadcasted_iota(I32, (groups, SUBLANES, LANES), 1)
    d = 1
    while d < SUBLANES:
        keep = sub >= d
        a_sh = jnp.where(keep, pltpu.roll(a, d, axis=1), 1.0)
        b_sh = jnp.where(keep, pltpu.roll(b, d, axis=1), 0.0)
        b = a * b_sh + b
        a = a * a_sh
        d *= 2
    a = a.reshape(tt, LANES)
    b = b.reshape(tt, LANES)
    carry = carry_ref[...]
    hs = []
    for g in range(tt // SUBLANES):
        rows = slice(g * SUBLANES, (g + 1) * SUBLANES)
        hs.append(a[rows, :] * carry + b[rows, :])
        carry = hs[-1][SUBLANES - 1:SUBLANES, :]
    carry_ref[...] = carry
    o_ref[...] = (jnp.concatenate(hs, axis=0) * _gelu_tanh(gate_ref[...])).astype(o_ref.dtype)


def rglru(proj, conv_w, conv_b, rg_w, rg_b, ig_w, ig_b, lam, *, tt=1024):
    t_len = proj.shape[0]
    nblk = rg_w.shape[0]
    width = nblk * LANES
    vec = lambda v: v.reshape(1, width)
    chan = lambda rows: pl.BlockSpec((rows, LANES), lambda n, t: (0, n))
    return pl.pallas_call(
        functools.partial(_rglru_kernel, tt=tt),
        out_shape=jax.ShapeDtypeStruct((t_len, width), BF16),
        grid=(nblk, t_len // tt),
        in_specs=[pl.BlockSpec((tt, LANES), lambda n, t: (t, n)),
                  pl.BlockSpec((tt, LANES), lambda n, t: (t, nblk + n)),
                  chan(CONV_WIDTH), chan(1),
                  pl.BlockSpec((None, LANES, LANES), lambda n, t: (n, 0, 0)), chan(1),
                  pl.BlockSpec((None, LANES, LANES), lambda n, t: (n, 0, 0)), chan(1),
                  chan(1)],
        out_specs=pl.BlockSpec((tt, LANES), lambda n, t: (t, n)),
        scratch_shapes=[pltpu.VMEM((SUBLANES, LANES), F32), pltpu.VMEM((1, LANES), F32)],
        compiler_params=_params(2),
        name="rglru",
    )(proj, proj, conv_w, vec(conv_b), rg_w.astype(BF16), vec(rg_b), ig_w.astype(BF16), vec(ig_b), vec(lam))


def _sb_kernel(q_ref, k_ref, v_ref, o_ref, acc_ref, run_ref, *, tq):
    i = pl.program_id(1)
    scale = HEAD_DIM ** -0.5
    rows = lax.broadcasted_iota(I32, (tq, tq), 0)
    cols = lax.broadcasted_iota(I32, (tq, tq), 1)
    later = (rows > cols).astype(BF16)
    causal = cols < rows
    heads = range(SB_HEADS_PER_STEP)

    def rows_of(j):
        return pl.ds(j * tq if isinstance(j, int) else pl.multiple_of(j * tq, tq), tq)

    def lanes_of(hd):
        return slice(hd * HEAD_DIM, (hd + 1) * HEAD_DIM)

    def front(hd, j, masked):
        z = lax.dot_general(q_ref[:, lanes_of(hd)], k_ref[rows_of(j), lanes_of(hd)], (((1,), (1,)), ((), ())),
                            preferred_element_type=F32) * scale
        sp = jnp.maximum(z, 0.0) + jnp.log(1.0 + jnp.exp(-jnp.abs(z)))
        log_keep = -sp
        log_beta = z - sp
        if masked:
            log_keep = jnp.where(causal, log_keep, 0.0)
        hi = log_keep.astype(BF16)
        lo = (log_keep - hi.astype(F32)).astype(BF16)
        after = (jnp.dot(hi, later, preferred_element_type=F32)
                 + jnp.dot(lo, later, preferred_element_type=F32))
        return log_beta + after, after[:, 0:1] + log_keep[:, 0:1]

    def back(hd, j, logw, run, masked):
        w = jnp.exp(logw if run is None else logw + run)
        if masked:
            w = jnp.where(causal, w, 0.0)
        return jnp.dot(w.astype(BF16), v_ref[rows_of(j), lanes_of(hd)], preferred_element_type=F32)

    @pl.when(i == 0)
    def _():
        for hd in heads:
            logw, _ = front(hd, 0, True)
            o_ref[:, lanes_of(hd)] = back(hd, 0, logw, None, True).astype(o_ref.dtype)

    @pl.when(i > 0)
    def _():
        tops = []
        for hd in heads:
            logw_d, tot_d = front(hd, i, True)
            logw_p, tot_p = front(hd, i - 1, False)
            acc_ref[:, lanes_of(hd)] = back(hd, i, logw_d, None, True) + back(hd, i - 1, logw_p, tot_d, False)
            run0 = tot_d + tot_p
            run_ref[hd] = run0
            tops.append(jnp.max(run0))

        def cond(state):
            j, top = state
            return jnp.logical_and(j >= 0, top > -SB_SKIP)

        def body(state):
            j, _ = state
            tops = []
            for hd in heads:
                logw, tot = front(hd, j, False)
                run = run_ref[hd]
                acc_ref[:, lanes_of(hd)] += back(hd, j, logw, run, False)
                run = run + tot
                run_ref[hd] = run
                tops.append(jnp.max(run))
            return j - 1, functools.reduce(jnp.maximum, tops)

        lax.while_loop(cond, body, (i - 2, functools.reduce(jnp.maximum, tops)))
        o_ref[...] = acc_ref[...].astype(o_ref.dtype)


def stick_breaking(qkv, *, n_heads, q_col, k_col, v_col, tq=256):
    t_len = qkv.shape[0]
    hg = SB_HEADS_PER_STEP
    width = hg * HEAD_DIM
    assert n_heads % hg == 0 and q_col % hg == 0 and k_col % hg == 0 and v_col % hg == 0
    return pl.pallas_call(
        functools.partial(_sb_kernel, tq=tq),
        out_shape=jax.ShapeDtypeStruct((t_len, n_heads * HEAD_DIM), BF16),
        grid=(n_heads // hg, t_len // tq),
        in_specs=[pl.BlockSpec((tq, width), lambda h, i: (i, q_col // hg + h)),
                  pl.BlockSpec((t_len, width), lambda h, i: (0, k_col // hg + h)),
                  pl.BlockSpec((t_len, width), lambda h, i: (0, v_col // hg + h))],
        out_specs=pl.BlockSpec((tq, width), lambda h, i: (i, h)),
        scratch_shapes=[pltpu.VMEM((tq, width), F32), pltpu.VMEM((hg, tq, 1), F32)],
        compiler_params=_params(2),
        name="stick_breaking",
    )(qkv, qkv, qkv)


def _layer_norm(y, g, b):
    mu = jnp.mean(y, axis=-1, keepdims=True)
    yc = y - mu
    var = jnp.mean(yc * yc, axis=-1, keepdims=True)
    return yc * lax.rsqrt(var + LN_EPS) * g + b


def _mixout_kernel(main_ref, qm_ref, mkv_ref, wout_ref, h_ref, g_ref, b_ref, rw2_ref, rb_ref,
                   o_ref, logit_ref, pack_ref, *, alpha):
    main_w = main_ref.shape[1]
    mix = jnp.dot(main_ref[...], wout_ref[0:main_w, :], preferred_element_type=F32)
    q = qm_ref[...].astype(BF16)
    heads = []
    for hd in range(MEM_HEADS):
        lo, hi = hd * HEAD_DIM, (hd + 1) * HEAD_DIM
        kh = mkv_ref[:, lo:hi]
        vh = mkv_ref[:, MEM_WIDTH + lo:MEM_WIDTH + hi]
        s = lax.dot_general(q[:, lo:hi], kh, (((1,), (1,)), ((), ())),
                            preferred_element_type=F32) * (HEAD_DIM ** -0.5)
        e = jnp.exp(s - jnp.max(s, axis=-1, keepdims=True))
        p = e * (1.0 / jnp.sum(e, axis=-1, keepdims=True))
        heads.append(jnp.dot(p.astype(BF16), vh, preferred_element_type=F32).astype(BF16))
    mem_out = jnp.concatenate(heads, axis=-1)
    mix = mix + jnp.dot(mem_out, wout_ref[main_w:, :], preferred_element_type=F32)
    h1 = _layer_norm(alpha * h_ref[...] + mix, g_ref[...], b_ref[...])
    o_ref[...] = h1
    hh = h1.astype(BF16)
    hh32 = hh.astype(F32)
    hl = (h1 - hh32).astype(BF16)
    both = jnp.dot(hh, rw2_ref[...], preferred_element_type=F32)
    logit_ref[...] = (both[:, :LANES] + both[:, LANES:]
                      + jnp.dot(hl, rw2_ref[:, :LANES], preferred_element_type=F32) + rb_ref[...])
    bits = pltpu.bitcast(hh32, jnp.uint32)
    half = bits.shape[1] // 2
    packed = bits[:, :half] | (bits[:, half:] >> 16)
    pack_ref[...] = packed.reshape(packed.shape[0], half // LANES, LANES)


def mixout(main, qsrc, q_col, mem_kv, w_out, h, ln_g, ln_b, router_w, router_b, *, alpha, tm=512):
    t_len, d = h.shape
    main_w = main.shape[1]
    n_e = router_w.shape[1]
    rw = jnp.zeros((d, LANES), F32).at[:, :n_e].set(router_w)
    rwh = rw.astype(BF16)
    rw2 = jnp.concatenate([rwh, (rw - rwh.astype(F32)).astype(BF16)], axis=1)
    rb = jnp.full((1, LANES), -jnp.inf, F32).at[0, :n_e].set(router_b)
    full = lambda a: pl.BlockSpec(a.shape, lambda i: (0, 0))
    args = (main, qsrc, mem_kv, w_out.astype(BF16), h, ln_g.reshape(1, d), ln_b.reshape(1, d), rw2, rb)
    return pl.pallas_call(
        functools.partial(_mixout_kernel, alpha=alpha),
        out_shape=(jax.ShapeDtypeStruct((t_len, d), F32), jax.ShapeDtypeStruct((t_len, LANES), F32),
                   jax.ShapeDtypeStruct((t_len, d // 2 // LANES, LANES), jnp.uint32)),
        grid=(t_len // tm,),
        in_specs=[pl.BlockSpec((tm, main_w), lambda i: (i, 0)),
                  pl.BlockSpec((tm, MEM_WIDTH), lambda i: (i, q_col)),
                  full(mem_kv), full(args[3]),
                  pl.BlockSpec((tm, d), lambda i: (i, 0)),
                  full(args[5]), full(args[6]), full(rw2), full(rb)],
        out_specs=(pl.BlockSpec((tm, d), lambda i: (i, 0)), pl.BlockSpec((tm, LANES), lambda i: (i, 0)),
                   pl.BlockSpec((tm, d // 2 // LANES, LANES), lambda i: (i, 0, 0))),
        compiler_params=_params(1),
        name="mixout",
    )(*args)


def _router_kernel(logit_ref, route_ref, cnt_ref, carry_ref, *, tb):
    @pl.when(pl.program_id(0) == 0)
    def _():
        carry_ref[...] = jnp.zeros_like(carry_ref)

    vals = logit_ref[...]
    lane = lax.broadcasted_iota(I32, (tb, LANES), 1).astype(F32)
    sels, tops, ids = [], [], []
    for _ in range(TOP_K):
        m = jnp.max(vals, axis=-1, keepdims=True)
        idx = jnp.min(jnp.where(vals == m, lane, float(LANES)), axis=-1, keepdims=True)
        sel = lane == idx
        vals = jnp.where(sel, -jnp.inf, vals)
        sels.append(sel)
        tops.append(m)
        ids.append(idx)
    exps = [jnp.exp(m - tops[0]) for m in tops]
    inv = 1.0 / (exps[0] + exps[1] + exps[2] + exps[3])
    onehot = jnp.zeros((tb, LANES), F32)
    for sel in sels:
        onehot = onehot + sel.astype(F32)
    rows = lax.broadcasted_iota(I32, (tb, tb), 0)
    cols = lax.broadcasted_iota(I32, (tb, tb), 1)
    before = (cols < rows).astype(BF16)
    cum = jnp.dot(before, onehot.astype(BF16), preferred_element_type=F32) + carry_ref[...]
    out = jnp.zeros((tb, LANES), F32)
    for k in range(TOP_K):
        pos = jnp.sum(jnp.where(sels[k], cum, 0.0), axis=-1, keepdims=True)
        out = jnp.where(lane == k, ids[k], out)
        out = jnp.where(lane == TOP_K + k, exps[k] * inv, out)
        out = jnp.where(lane == 2 * TOP_K + k, pos, out)
    route_ref[...] = out
    carry_ref[...] += jnp.sum(onehot, axis=0, keepdims=True)
    cnt_ref[...] = jnp.broadcast_to(carry_ref[...], cnt_ref.shape)


def router(logits, *, tb=512):
    t_len = logits.shape[0]
    return pl.pallas_call(
        functools.partial(_router_kernel, tb=tb),
        out_shape=(jax.ShapeDtypeStruct((t_len, LANES), F32), jax.ShapeDtypeStruct((SUBLANES, LANES), F32)),
        grid=(t_len // tb,),
        in_specs=[pl.BlockSpec((tb, LANES), lambda i: (i, 0))],
        out_specs=(pl.BlockSpec((tb, LANES), lambda i: (i, 0)), pl.BlockSpec((SUBLANES, LANES), lambda i: (0, 0))),
        scratch_shapes=[pltpu.VMEM((1, LANES), F32)],
        compiler_params=_params(1),
        name="router",
    )(logits)


def _dispatch_kernel(te_ref, na_ref, dest_ref, h_ref, xs_ref, zero_ref, sem, zsem, *, tb, tm, n_tiles):
    @pl.when(pl.program_id(0) == 0)
    def _():
        zero_ref[...] = jnp.zeros_like(zero_ref)

        def has_padding(j):
            return jnp.logical_or(j >= na_ref[0] - 1, te_ref[j] != te_ref[jnp.minimum(j + 1, n_tiles - 1)])

        def zero_copy(j):
            return pltpu.make_async_copy(zero_ref, xs_ref.at[pl.ds(pl.multiple_of(j * tm, tm), tm)], zsem)

        def zstart(j, c):
            @pl.when(has_padding(j))
            def _():
                zero_copy(j).start()
            return c

        def zwait(j, c):
            @pl.when(has_padding(j))
            def _():
                zero_copy(j).wait()
            return c

        lax.fori_loop(0, n_tiles, zstart, 0)
        lax.fori_loop(0, n_tiles, zwait, 0)

    def start(r, c):
        for k in range(TOP_K):
            slot = dest_ref[0, 0, r * TOP_K + k]
            pltpu.make_async_copy(h_ref.at[pl.ds(r, 1)], xs_ref.at[pl.ds(slot, 1)], sem).start(priority=k % 2)
        return c

    lax.fori_loop(0, tb, start, 0)
    rows = xs_ref.at[pl.ds(0, tb * TOP_K)]
    pltpu.make_async_copy(rows, rows, sem).wait()


def dispatch(h, dest, tile_e, n_active, n_rows, *, tm, tb=256):
    t_len, slab = h.shape[0], h.shape[1:]
    dest3 = dest.reshape(t_len // tb, 1, tb * TOP_K)
    return pl.pallas_call(
        functools.partial(_dispatch_kernel, tb=tb, tm=tm, n_tiles=n_rows // tm),
        out_shape=jax.ShapeDtypeStruct((n_rows,) + slab, h.dtype),
        grid_spec=pltpu.PrefetchScalarGridSpec(
            num_scalar_prefetch=2,
            grid=(t_len // tb,),
            in_specs=[pl.BlockSpec((1, 1, tb * TOP_K), lambda i, te, na: (i, 0, 0), memory_space=pltpu.SMEM),
                      pl.BlockSpec((tb,) + slab, lambda i, te, na: (i, 0, 0))],
            out_specs=pl.BlockSpec(memory_space=pl.ANY),
            scratch_shapes=[pltpu.VMEM((tm,) + slab, h.dtype), pltpu.SemaphoreType.DMA(()),
                            pltpu.SemaphoreType.DMA(())]),
        compiler_params=_params(1),
        name="dispatch",
    )(tile_e, n_active, dest3, h)


def _tile_idx(i, te_ref, na_ref):
    return jnp.minimum(i, na_ref[0] - 1)


def _zero_inactive(i, na_ref, o_ref):
    @pl.when(i >= na_ref[0])
    def _():
        o_ref[...] = jnp.zeros_like(o_ref)


def _expert_changed(i, te_ref):
    return jnp.logical_or(i == 0, te_ref[i] != te_ref[jnp.maximum(i - 1, 0)])


def _gateup_kernel(te_ref, na_ref, nx_ref, nv_ref, x_ref, w_hbm, b_ref, o_ref, stage_ref, wb_ref, xb_ref, sem):
    i = pl.program_id(0)
    nh = stage_ref.shape[1]
    _zero_inactive(i, na_ref, o_ref)

    def wcopy(e, hf):
        return pltpu.make_async_copy(w_hbm.at[e, :, pl.ds(hf * nh, nh)], stage_ref, sem)

    def reorder(hf):
        rows = lax.broadcasted_iota(I32, (GU, GU), 0)
        cols = lax.broadcasted_iota(I32, (GU, GU), 1)
        src = jnp.where(cols < LANES, 2 * cols, 2 * (cols - LANES) + 1)
        perm = (rows == src).astype(BF16)
        for n in range(0, nh, GU):
            wb_ref[:, hf * nh + n:hf * nh + n + GU] = jnp.dot(
                stage_ref[:, n:n + GU].astype(BF16), perm, preferred_element_type=F32).astype(BF16)

    def columns_of(hf, rows):
        x = xb_ref[0:rows, :]
        for n in range(hf * nh, (hf + 1) * nh, GU):
            gu = jnp.dot(x, wb_ref[:, n:n + GU], preferred_element_type=F32) + b_ref[:, n:n + GU]
            gate = jnp.minimum(gu[:, :LANES], SWIGLU_LIMIT)
            up = jnp.clip(gu[:, LANES:], -SWIGLU_LIMIT, SWIGLU_LIMIT)
            act = (up + 1.0) * gate * jax.nn.sigmoid(SWIGLU_ALPHA * gate)
            o_ref[0:rows, n // 2:n // 2 + LANES] = act.astype(o_ref.dtype)

    def columns(hf):
        tm = o_ref.shape[0]
        quarter = tm // ROW_QUARTERS
        used = (nv_ref[i] + quarter - 1) // quarter
        for q in range(1, ROW_QUARTERS + 1):
            @pl.when(used == q if q > 1 else used <= 1)
            def _():
                columns_of(hf, q * quarter)
                if q < ROW_QUARTERS:
                    o_ref[q * quarter:, hf * nh // 2:(hf + 1) * nh // 2] = jnp.zeros(
                        (tm - q * quarter, nh // 2), o_ref.dtype)

    @pl.when(i < na_ref[0])
    def _():
        changed = _expert_changed(i, te_ref)
        e = te_ref[i]
        words = x_ref[...].reshape(x_ref.shape[0], -1)
        upper = pltpu.bitcast(words & jnp.uint32(0xFFFF0000), F32)
        lower = pltpu.bitcast(words << 16, F32)
        xb_ref[...] = jnp.concatenate([upper, lower], axis=1).astype(BF16)

        @pl.when(i == 0)
        def _():
            wcopy(e, 0).start()

        @pl.when(changed)
        def _():
            wcopy(e, 0).wait()
            reorder(0)
            wcopy(e, 1).start()

        columns(0)

        @pl.when(changed)
        def _():
            wcopy(e, 1).wait()
            reorder(1)

            @pl.when(nx_ref[i] >= 0)
            def _():
                wcopy(nx_ref[i], 0).start()

        columns(1)


def moe_gateup(tile_e, n_active, next_e, n_valid, x, w, b, *, tm):
    n_rows = x.shape[0]
    k, n2 = w.shape[1], w.shape[2]
    return pl.pallas_call(
        _gateup_kernel,
        out_shape=jax.ShapeDtypeStruct((n_rows, n2 // 2), BF16),
        grid_spec=pltpu.PrefetchScalarGridSpec(
            num_scalar_prefetch=4,
            grid=(n_rows // tm,),
            in_specs=[pl.BlockSpec((tm,) + x.shape[1:], lambda i, te, na, nx, nv: (_tile_idx(i, te, na), 0, 0)),
                      pl.BlockSpec(memory_space=pl.ANY),
                      pl.BlockSpec((None, 1, n2), lambda i, te, na, nx, nv: (te[_tile_idx(i, te, na)], 0, 0))],
            out_specs=pl.BlockSpec((tm, n2 // 2), lambda i, te, na, nx, nv: (i, 0)),
            scratch_shapes=[pltpu.VMEM((k, n2 // 2), F32), pltpu.VMEM((k, n2), BF16),
                            pltpu.VMEM((tm, k), BF16), pltpu.SemaphoreType.DMA(())]),
        compiler_params=_params(1),
        name="moe_gateup",
    )(tile_e, n_active, next_e, n_valid, x, w, b)


def _down_kernel(te_ref, na_ref, nx_ref, nv_ref, x_ref, w_hbm, b_ref, o_ref, stage_ref, wb_ref, sem, *, tn):
    i = pl.program_id(0)
    _zero_inactive(i, na_ref, o_ref)

    def wcopy(e):
        return pltpu.make_async_copy(w_hbm.at[e], stage_ref, sem)

    @pl.when(i < na_ref[0])
    def _():
        e = te_ref[i]

        @pl.when(i == 0)
        def _():
            wcopy(e).start()

        @pl.when(_expert_changed(i, te_ref))
        def _():
            wcopy(e).wait()
            wb_ref[...] = stage_ref[...].astype(BF16)

            @pl.when(nx_ref[i] >= 0)
            def _():
                wcopy(nx_ref[i]).start()

        def rows_of(rows):
            for r0 in range(0, rows, ROW_CHUNK):
                rsub = min(ROW_CHUNK, rows - r0)
                x = x_ref[r0:r0 + rsub, :]
                for n in range(0, wb_ref.shape[1], tn):
                    y = jnp.dot(x, wb_ref[:, n:n + tn], preferred_element_type=F32) + b_ref[:, n:n + tn]
                    o_ref[r0:r0 + rsub, n // LANES:(n + tn) // LANES, :] = y.reshape(rsub, tn // LANES, LANES)

        tm = o_ref.shape[0]
        quarter = tm // ROW_QUARTERS
        used = (nv_ref[i] + quarter - 1) // quarter
        for q in range(1, ROW_QUARTERS + 1):
            @pl.when(used == q if q > 1 else used <= 1)
            def _():
                rows_of(q * quarter)
                if q < ROW_QUARTERS:
                    o_ref[q * quarter:] = jnp.zeros((tm - q * quarter,) + o_ref.shape[1:], o_ref.dtype)


def moe_down(tile_e, n_active, next_e, n_valid, x, w, b, *, tm, tn=SUBLANES * LANES):
    n_rows, k = x.shape
    n_out = w.shape[2]
    return pl.pallas_call(
        functools.partial(_down_kernel, tn=tn),
        out_shape=jax.ShapeDtypeStruct((n_rows, n_out // LANES, LANES), F32),
        grid_spec=pltpu.PrefetchScalarGridSpec(
            num_scalar_prefetch=4,
            grid=(n_rows // tm,),
            in_specs=[pl.BlockSpec((tm, k), lambda i, te, na, nx, nv: (_tile_idx(i, te, na), 0)),
                      pl.BlockSpec(memory_space=pl.ANY),
                      pl.BlockSpec((None, 1, n_out), lambda i, te, na, nx, nv: (te[_tile_idx(i, te, na)], 0, 0))],
            out_specs=pl.BlockSpec((tm, n_out // LANES, LANES), lambda i, te, na, nx, nv: (i, 0, 0)),
            scratch_shapes=[pltpu.VMEM((k, n_out), F32), pltpu.VMEM((k, n_out), BF16),
                            pltpu.SemaphoreType.DMA(())]),
        compiler_params=_params(1),
        name="moe_down",
    )(tile_e, n_active, next_e, n_valid, x, w, b)


def _combine_kernel(dest_ref, next_ref, gate_ref, h_ref, g_ref, b_ref, y_ref, o_ref, buf_ref, sems, *, tb, alpha):
    i = pl.program_id(0)

    def gather(idx_ref, s):
        def start(r, c):
            for k in range(TOP_K):
                row = idx_ref[0, 0, r * TOP_K + k]
                pltpu.make_async_copy(y_ref.at[pl.ds(row, 1)], buf_ref.at[s, k, pl.ds(r, 1)],
                                      sems.at[s]).start(priority=k % 2)
            return c
        lax.fori_loop(0, tb, start, 0)

    @pl.when(i == 0)
    def _():
        gather(dest_ref, 0)

    for s in range(2):
        @pl.when(i % 2 == s)
        def _():
            @pl.when(i + 1 < pl.num_programs(0))
            def _():
                gather(next_ref, 1 - s)

            pltpu.make_async_copy(buf_ref.at[s], buf_ref.at[s], sems.at[s]).wait()
            gates = gate_ref[...]
            d = h_ref.shape[1]
            ffn = buf_ref[s, 0].reshape(tb, d) * gates[:, TOP_K:TOP_K + 1]
            for k in range(1, TOP_K):
                ffn = ffn + buf_ref[s, k].reshape(tb, d) * gates[:, TOP_K + k:TOP_K + k + 1]
            o_ref[...] = _layer_norm(alpha * h_ref[...] + ffn, g_ref[...], b_ref[...])


def combine(y, dest, route, h, ln_g, ln_b, *, alpha, tb=256):
    t_len, d = h.shape
    nb = t_len // tb
    dest3 = dest.reshape(nb, 1, tb * TOP_K)
    idx_spec = lambda f: pl.BlockSpec((1, 1, tb * TOP_K), f, memory_space=pltpu.SMEM)
    return pl.pallas_call(
        functools.partial(_combine_kernel, tb=tb, alpha=alpha),
        out_shape=jax.ShapeDtypeStruct((t_len, d), F32),
        grid=(nb,),
        in_specs=[idx_spec(lambda i: (i, 0, 0)),
                  idx_spec(lambda i: (jnp.minimum(i + 1, nb - 1), 0, 0)),
                  pl.BlockSpec((tb, LANES), lambda i: (i, 0)),
                  pl.BlockSpec((tb, d), lambda i: (i, 0)),
                  pl.BlockSpec((1, d), lambda i: (0, 0)),
                  pl.BlockSpec((1, d), lambda i: (0, 0)),
                  pl.BlockSpec(memory_space=pl.ANY)],
        out_specs=pl.BlockSpec((tb, d), lambda i: (i, 0)),
        scratch_shapes=[pltpu.VMEM((2, TOP_K, tb) + y.shape[1:], F32), pltpu.SemaphoreType.DMA((2,))],
        compiler_params=_params(1),
        name="combine",
    )(dest3, dest3, route, h, ln_g.reshape(1, d), ln_b.reshape(1, d), y)


def moe(h1, h1_packed, logits, layer, w_gate_up, b_gate_up, w_down, b_down, ln_g, ln_b, *, alpha, tm=512):
    t_len, d = h1.shape
    n_e = w_gate_up.shape[1]
    route, cnt = router(logits)
    eid = route[:, 0:TOP_K].astype(I32)
    pos = route[:, 2 * TOP_K:3 * TOP_K].astype(I32)
    counts = cnt[0, :n_e].astype(I32)
    padded = (counts + tm - 1) // tm * tm
    pend = jnp.cumsum(padded)
    pstart = pend - padded
    onehot = eid[:, :, None] == jnp.arange(n_e, dtype=I32)[None, None, :]
    dest = jnp.sum(jnp.where(onehot, pstart[None, None, :], 0), axis=-1) + pos
    n_tiles = t_len * TOP_K // tm + n_e
    tile_start = jnp.arange(n_tiles, dtype=I32) * tm
    tile_e = jnp.minimum(jnp.sum((pend[None, :] <= tile_start[:, None]).astype(I32), axis=1), n_e - 1)
    n_active = (pend[-1:] // tm).astype(I32)

    xs = dispatch(h1_packed, dest, tile_e, n_active, n_tiles * tm, tm=tm)
    bgu = b_gate_up.reshape(-1, n_e, b_gate_up.shape[-1] // GU, LANES, 2).swapaxes(-1, -2)
    bgu = bgu.reshape(-1, 1, b_gate_up.shape[-1])
    tile_w = tile_e + layer * n_e
    tidx = jnp.arange(n_tiles, dtype=I32)
    follows = ((tile_w[None, :] != tile_w[:, None]) & (tidx[None, :] > tidx[:, None])
               & (tidx[None, :] < n_active[0]))
    next_e = jnp.where(jnp.any(follows, axis=1), tile_w[jnp.argmax(follows, axis=1)], -1).astype(I32)
    group_end = (pstart + counts)[tile_e]
    n_valid = jnp.clip(group_end - tile_start, 0, tm).astype(I32)
    act = moe_gateup(tile_w, n_active, next_e, n_valid, xs, w_gate_up.reshape((-1,) + w_gate_up.shape[2:]),
                     bgu, tm=tm)
    y = moe_down(tile_w, n_active, next_e, n_valid, act, w_down.reshape((-1,) + w_down.shape[2:]),
                 b_down.reshape(-1, 1, b_down.shape[-1]), tm=tm)
    return combine(y, dest, route, h1, ln_g, ln_b, alpha=alpha)


def kernel(x, mem, a_w_in, a_conv_w, a_conv_b, a_rg_w, a_rg_b, a_ig_w, a_ig_b, a_lambda, a_w_out, b_w_q, b_w_out, w_kv_shared, mem_w_kv, ln1_g, ln1_b, ln2_g, ln2_b, router_w, router_b, w_gate_up, b_gate_up, w_down, b_down):
    bsz, seq, d = x.shape
    depth = ln1_g.shape[0]
    n_a = a_w_in.shape[0]
    alpha = float((2 * depth) ** 0.25)
    outs = []
    for bi in range(bsz):
        h = x[bi]
        mem_b = mem[bi]
        qkv = None
        for layer in range(depth):
            mem_kv = dense(mem_b, mem_w_kv[layer].astype(BF16), BF16, tm=mem_b.shape[0])
            if layer < n_a:
                lw = a_rg_w.shape[1] * LANES
                proj = dense(h, a_w_in[layer].astype(BF16), F32, tm=512)
                main = rglru(proj, a_conv_w[layer], a_conv_b[layer], a_rg_w[layer], a_rg_b[layer],
                             a_ig_w[layer], a_ig_b[layer], a_lambda[layer])
                qsrc, q_col, w_out = proj, 2 * lw // MEM_WIDTH, a_w_out[layer]
            else:
                j = layer - n_a
                sbw = w_kv_shared.shape[1] // 2
                n_heads = sbw // HEAD_DIM
                if j == 0:
                    w_cat = jnp.concatenate([w_kv_shared, b_w_q[j]], axis=1).astype(BF16)
                    qkv = dense(h, w_cat, BF16, tm=512)
                else:
                    q_only = dense(h, b_w_q[j].astype(BF16), BF16, tm=512)
                    qkv = jnp.concatenate([qkv[:, :2 * sbw], q_only], axis=1)
                main = stick_breaking(qkv, n_heads=n_heads, q_col=2 * n_heads, k_col=0, v_col=n_heads)
                qsrc, q_col, w_out = qkv, (3 * sbw) // MEM_WIDTH, b_w_out[j]
            h1, logits, h1_packed = mixout(main, qsrc, q_col, mem_kv, w_out, h, ln1_g[layer], ln1_b[layer],
                                           router_w[layer], router_b[layer], alpha=alpha)
            h = moe(h1, h1_packed, logits, layer, w_gate_up, b_gate_up, w_down, b_down,
                    ln2_g[layer], ln2_b[layer], alpha=alpha)
        outs.append(h)
    return jnp.stack(outs, axis=0)
```
